```python
import math
import jax, jax.numpy as jnp
from jax import lax
import numpy as np


D_MODEL = 2048
BATCH = 4
SEQ = 2048
DEPTH = 4
DEC_BATCH = 128
DEC_SEQ = 1
PAST_LEN = 8192
PAGE_SIZE = 128

D_SSM = D_MODEL // 2
GROUP_CH = 16
N_GROUPS = D_SSM // GROUP_CH
STATE_P = 64
N_HEADS = D_MODEL // 128
Q_LORA = D_MODEL // 4
KV_LORA = D_MODEL // 4
NOPE_DIM = 128
ROPE_DIM = 64
V_DIM = 128
QK_DIM = NOPE_DIM + ROPE_DIM
KV_ROW = KV_LORA + ROPE_DIM
ROPE_THETA = 10000.0
Q_BLOCK = 128
D_FF = 11 * D_MODEL // 4
CONV_W = 3
EPS = 1e-6
SPLITS = (D_SSM, D_SSM + Q_LORA, D_SSM + Q_LORA + KV_LORA,
          D_SSM + Q_LORA + KV_LORA + ROPE_DIM,
          D_SSM + Q_LORA + KV_LORA + ROPE_DIM + D_MODEL)
IN_COLS = D_SSM + Q_LORA + KV_LORA + ROPE_DIM + 2 * D_MODEL

kernel_name = 'hybrid_s5_mla_adaln_convffn_step'


def rmsnorm(x, g):
    xf = x.astype(jnp.float32)
    xf = xf * lax.rsqrt(jnp.mean(xf * xf, axis=-1, keepdims=True) + EPS)
    return xf.astype(x.dtype) * g


def ada_modulation(c, w, b):
    m = jax.nn.silu(c) @ w + b
    return jnp.split(m[:, None, :], 6, axis=-1)


def modulate(x, g, shift, scale):
    return rmsnorm(x, g) * (1.0 + scale) + shift


def rope_tables(pos):
    half = ROPE_DIM // 2
    inv = ROPE_THETA ** (-jnp.arange(half, dtype=jnp.float32) / half)
    ang = pos.astype(jnp.float32)[:, None] * inv[None, :]
    return jnp.cos(ang), jnp.sin(ang)


def apply_rope(x, cos, sin):
    x1, x2 = jnp.split(x, 2, axis=-1)
    cos = cos.astype(x.dtype)
    sin = sin.astype(x.dtype)
    return jnp.concatenate([x1 * cos - x2 * sin, x1 * sin + x2 * cos], axis=-1)


def _complex_affine_combine(e1, e2):
    a1r, a1i, b1r, b1i = e1
    a2r, a2i, b2r, b2i = e2
    return (a2r * a1r - a2i * a1i, a2r * a1i + a2i * a1r,
            a2r * b1r - a2i * b1i + b2r, a2r * b1i + a2i * b1r + b2i)


def ssm_branch(u, h0_re, h0_im, lam_re, lam_im, log_step, b_re, b_im,
               c_re, c_im, d_skip, w_glu, b_glu):
    bn, t, _ = u.shape
    ug = u.reshape(bn, t, N_GROUPS, GROUP_CH)
    dt = jnp.exp(log_step)[:, None]
    decay = jnp.exp(lam_re * dt)
    a_re = decay * jnp.cos(lam_im * dt)
    a_im = decay * jnp.sin(lam_im * dt)
    den = lam_re * lam_re + lam_im * lam_im
    f_re = ((a_re - 1.0) * lam_re + a_im * lam_im) / den
    f_im = (a_im * lam_re - (a_re - 1.0) * lam_im) / den
    bb_re = f_re[..., None] * b_re - f_im[..., None] * b_im
    bb_im = f_re[..., None] * b_im + f_im[..., None] * b_re
    bu_re = jnp.einsum('btgc,gpc->btgp', ug, bb_re)
    bu_im = jnp.einsum('btgc,gpc->btgp', ug, bb_im)
    bu_re = bu_re.at[:, 0].add(a_re * h0_re - a_im * h0_im)
    bu_im = bu_im.at[:, 0].add(a_re * h0_im + a_im * h0_re)
    ar = jnp.broadcast_to(a_re, bu_re.shape)
    ai = jnp.broadcast_to(a_im, bu_im.shape)
    _, _, h_re, h_im = lax.associative_scan(_complex_affine_combine,
                                            (ar, ai, bu_re, bu_im), axis=1)
    y = (jnp.einsum('btgp,gcp->btgc', h_re, c_re)
         - jnp.einsum('btgp,gcp->btgc', h_im, c_im))
    y = y.reshape(bn, t, D_SSM) + d_skip * u
    z = jax.nn.gelu(y)
    out = z * jax.nn.sigmoid(z @ w_glu + b_glu)
    return out, h_re[:, -1], h_im[:, -1]


def mla_queries(q_lat, g_q_lat, w_uq, g_qk_q, cos, sin):
    bn, t, _ = q_lat.shape
    q = (rmsnorm(q_lat, g_q_lat) @ w_uq).reshape(bn, t, N_HEADS, QK_DIM)
    q = jnp.concatenate([q[..., :NOPE_DIM],
                         apply_rope(q[..., NOPE_DIM:], cos[:, None, :], sin[:, None, :])], axis=-1)
    return rmsnorm(q, g_qk_q)


def mla_keys(c_kv, k_rope, w_uk, g_qk_k):
    k_nope = jnp.einsum('bkc,chd->bkhd', c_kv, w_uk)
    kr = jnp.broadcast_to(k_rope[:, :, None, :], k_nope.shape[:-1] + (ROPE_DIM,))
    return rmsnorm(jnp.concatenate([k_nope, kr], axis=-1), g_qk_k)


def mla_prompt(q, rows, w_uk, w_uv, g_qk_k):
    bn, t = q.shape[:2]
    c = rows[..., :KV_LORA]
    k = mla_keys(c, rows[..., KV_LORA:], w_uk, g_qk_k)
    v = jnp.einsum('bkc,chd->bkhd', c, w_uv)
    nb = t // Q_BLOCK
    qb = q.reshape(bn, nb, Q_BLOCK, N_HEADS, QK_DIM).transpose(1, 0, 2, 3, 4)
    kpos = jnp.arange(t)
    scale = QK_DIM ** -0.5

    def block(args):
        qblk, i = args
        s = jnp.einsum('bqhd,bkhd->bhqk', qblk, k).astype(jnp.float32) * scale
        qpos = i * Q_BLOCK + jnp.arange(Q_BLOCK)
        s = jnp.where(kpos[None, :] <= qpos[:, None], s, -jnp.inf)
        p = jax.nn.softmax(s, axis=-1).astype(v.dtype)
        return jnp.einsum('bhqk,bkhd->bqhd', p, v)

    o = lax.map(block, (qb, jnp.arange(nb)))
    return o.transpose(1, 0, 2, 3, 4).reshape(bn, t, N_HEADS * V_DIM)


def mla_sample(q, new_rows, cache_kv, layer, page_table, w_uk, w_uv, g_qk_k):
    bd, s_new = q.shape[:2]
    scale = QK_DIM ** -0.5

    def page_partial(phys):
        rows = cache_kv[layer, phys]
        c = rows[..., :KV_LORA]
        k = mla_keys(c, rows[..., KV_LORA:], w_uk, g_qk_k)
        s = jnp.einsum('bqhd,bkhd->bhqk', q, k).astype(jnp.float32) * scale
        m = jnp.max(s, axis=-1)
        p = jnp.exp(s - m[..., None])
        return m, jnp.sum(p, axis=-1), jnp.einsum('bhqk,bkc->bhqc', p, c.astype(jnp.float32))

    m_p, l_p, acc_p = lax.map(page_partial, page_table.T)
    c_new = new_rows[..., :KV_LORA]
    k_new = mla_keys(c_new, new_rows[..., KV_LORA:], w_uk, g_qk_k)
    s_n = jnp.einsum('bqhd,bkhd->bhqk', q, k_new).astype(jnp.float32) * scale
    causal = jnp.tril(jnp.ones((s_new, s_new), dtype=bool))
    s_n = jnp.where(causal, s_n, -jnp.inf)
    m = jnp.maximum(jnp.max(m_p, axis=0), jnp.max(s_n, axis=-1))
    w_p = jnp.exp(m_p - m[None])
    p_n = jnp.exp(s_n - m[..., None])
    l = jnp.sum(w_p * l_p, axis=0) + jnp.sum(p_n, axis=-1)
    acc = (jnp.sum(w_p[..., None] * acc_p, axis=0)
           + jnp.einsum('bhqk,bkc->bhqc', p_n, c_new.astype(jnp.float32)))
    o_lat = (acc / l[..., None]).astype(q.dtype)
    o = jnp.einsum('bhqc,chd->bqhd', o_lat, w_uv)
    return o.reshape(bd, s_new, N_HEADS * V_DIM)


def conv_ffn(h, buf, w_up, conv_w, conv_b, w_down):
    u = h @ w_up
    t = u.shape[1]
    full = jnp.concatenate([buf, u], axis=1)
    z = conv_b + full[:, 0:t] * conv_w[0]
    for k in range(1, CONV_W):
        z = z + full[:, k:k + t] * conv_w[k]
    g, v = jnp.split(z, 2, axis=-1)
    return (jax.nn.silu(g) * v) @ w_down, full[:, -(CONV_W - 1):]


def setup_inputs(seed: int = 0) -> dict:
    key = jax.random.key(seed)
    ks = iter(jax.random.split(key, 48))

    def nrm(shape, scale):
        return jax.random.normal(next(ks), shape, jnp.float32) * scale

    L = DEPTH
    n_pages = PAST_LEN // PAGE_SIZE
    n_used = DEC_BATCH * n_pages
    n_pool = n_used + n_used // 4
    x_prompt = nrm((BATCH, SEQ, D_MODEL), 1.0)
    x_sample = nrm((DEC_BATCH, DEC_SEQ, D_MODEL), 1.0)
    c_prompt = nrm((BATCH, D_MODEL), 1.0)
    c_sample = nrm((DEC_BATCH, D_MODEL), 1.0)
    cache_kv = nrm((L, n_pool, PAGE_SIZE, KV_ROW), 1.0)
    state_ssm_re = nrm((L, DEC_BATCH, N_GROUPS, STATE_P), 0.3)
    state_ssm_im = nrm((L, DEC_BATCH, N_GROUPS, STATE_P), 0.3)
    state_conv = nrm((L, DEC_BATCH, CONV_W - 1, 2 * D_FF), 1.0)
    page_table = jax.random.permutation(next(ks), n_pool)[:n_used].reshape(
        DEC_BATCH, n_pages).astype(jnp.int32)
    w_ada = nrm((L, D_MODEL, 6 * D_MODEL), D_MODEL ** -0.5)
    b_ada = nrm((L, 6 * D_MODEL), 0.01)
    g_norm_mix = 1.0 + nrm((L, D_MODEL), 0.02)
    g_norm_ffn = 1.0 + nrm((L, D_MODEL), 0.02)
    w_in = nrm((L, D_MODEL, IN_COLS), D_MODEL ** -0.5)
    lam_re = -0.5 + nrm((L, N_GROUPS, STATE_P), 0.01)
    lam_im = jnp.pi * jnp.arange(STATE_P, dtype=jnp.float32) + nrm((L, N_GROUPS, STATE_P), 0.01)
    log_step = jax.random.uniform(next(ks), (L, N_GROUPS), jnp.float32,
                                  math.log(1e-3), math.log(1e-1))
    b_re = nrm((L, N_GROUPS, STATE_P, GROUP_CH), (2 * GROUP_CH) ** -0.5)
    b_im = nrm((L, N_GROUPS, STATE_P, GROUP_CH), (2 * GROUP_CH) ** -0.5)
    c_re = nrm((L, N_GROUPS, GROUP_CH, STATE_P), STATE_P ** -0.5)
    c_im = nrm((L, N_GROUPS, GROUP_CH, STATE_P), STATE_P ** -0.5)
    d_skip = nrm((L, D_SSM), 1.0)
    w_glu = nrm((L, D_SSM, D_SSM), D_SSM ** -0.5)
    b_glu = nrm((L, D_SSM), 0.01)
    w_ssm_out = nrm((L, D_SSM, D_MODEL), D_SSM ** -0.5)
    g_q_lat = 1.0 + nrm((L, Q_LORA), 0.02)
    w_uq = nrm((L, Q_LORA, N_HEADS * QK_DIM), Q_LORA ** -0.5)
    g_kv_lat = 1.0 + nrm((L, KV_LORA), 0.02)
    w_uk = nrm((L, KV_LORA, N_HEADS, NOPE_DIM), KV_LORA ** -0.5)
    w_uv = nrm((L, KV_LORA, N_HEADS, V_DIM), KV_LORA ** -0.5)
    g_qk_q = 1.0 + nrm((L, QK_DIM), 0.02)
    g_qk_k = 1.0 + nrm((L, QK_DIM), 0.02)
    w_att_out = nrm((L, N_HEADS * V_DIM, D_MODEL), (N_HEADS * V_DIM) ** -0.5)
    w_o = nrm((L, D_MODEL, D_MODEL), D_MODEL ** -0.5)
    w_up = nrm((L, D_MODEL, 2 * D_FF), D_MODEL ** -0.5)
    conv_w = nrm((L, CONV_W, 2 * D_FF), CONV_W ** -0.5)
    conv_b = nrm((L, 2 * D_FF), 0.01)
    w_down = nrm((L, D_FF, D_MODEL), D_FF ** -0.5)
    return {'x_prompt': x_prompt, 'x_sample': x_sample, 'c_prompt': c_prompt, 'c_sample': c_sample,
            'cache_kv': cache_kv, 'state_ssm_re': state_ssm_re, 'state_ssm_im': state_ssm_im,
            'state_conv': state_conv, 'page_table': page_table,
            'w_ada': w_ada, 'b_ada': b_ada, 'g_norm_mix': g_norm_mix, 'g_norm_ffn': g_norm_ffn,
            'w_in': w_in, 'lam_re': lam_re, 'lam_im': lam_im, 'log_step': log_step,
            'b_re': b_re, 'b_im': b_im, 'c_re': c_re, 'c_im': c_im, 'd_skip': d_skip,
            'w_glu': w_glu, 'b_glu': b_glu, 'w_ssm_out': w_ssm_out,
            'g_q_lat': g_q_lat, 'w_uq': w_uq, 'g_kv_lat': g_kv_lat, 'w_uk': w_uk, 'w_uv': w_uv,
            'g_qk_q': g_qk_q, 'g_qk_k': g_qk_k, 'w_att_out': w_att_out,
            'w_o': w_o, 'w_up': w_up, 'conv_w': conv_w, 'conv_b': conv_b, 'w_down': w_down}


def reference(x_prompt, x_sample, c_prompt, c_sample, cache_kv, state_ssm_re, state_ssm_im,
              state_conv, page_table, w_ada, b_ada, g_norm_mix, g_norm_ffn, w_in,
              lam_re, lam_im, log_step, b_re, b_im, c_re, c_im, d_skip, w_glu, b_glu, w_ssm_out,
              g_q_lat, w_uq, g_kv_lat, w_uk, w_uv, g_qk_q, g_qk_k, w_att_out,
              w_o, w_up, conv_w, conv_b, w_down):
    past_len = page_table.shape[1] * PAGE_SIZE
    pos_p = jnp.arange(x_prompt.shape[1])
    pos_s = past_len + jnp.arange(x_sample.shape[1])

    def attend_prompt(q, rows, l):
        return mla_prompt(q, rows, w_uk[l], w_uv[l], g_qk_k[l])

    def attend_sample(q, rows, l):
        return mla_sample(q, rows, cache_kv, l, page_table, w_uk[l], w_uv[l], g_qk_k[l])

    def layer(l, x, c, pos, h0_re, h0_im, conv_buf, attend):
        sh1, sc1, gt1, sh2, sc2, gt2 = ada_modulation(c, w_ada[l], b_ada[l])
        h = modulate(x, g_norm_mix[l], sh1, sc1)
        u, q_lat, c_raw, kr_raw, g_a, g_b = jnp.split(h @ w_in[l], SPLITS, axis=-1)
        y_ssm, hn_re, hn_im = ssm_branch(u, h0_re, h0_im, lam_re[l], lam_im[l], log_step[l],
                                         b_re[l], b_im[l], c_re[l], c_im[l], d_skip[l],
                                         w_glu[l], b_glu[l])
        cos, sin = rope_tables(pos)
        q = mla_queries(q_lat, g_q_lat[l], w_uq[l], g_qk_q[l], cos, sin)
        rows = jnp.concatenate([rmsnorm(c_raw, g_kv_lat[l]), apply_rope(kr_raw, cos, sin)], axis=-1)
        o = attend(q, rows, l)
        merged = (jax.nn.sigmoid(g_a) * (y_ssm @ w_ssm_out[l])
                  + jax.nn.sigmoid(g_b) * (o @ w_att_out[l]))
        x = x + gt1 * (merged @ w_o[l])
        f, new_buf = conv_ffn(modulate(x, g_norm_ffn[l], sh2, sc2), conv_buf,
                              w_up[l], conv_w[l], conv_b[l], w_down[l])
        x = x + gt2 * f
        return x, rows, hn_re, hn_im, new_buf

    bp = x_prompt.shape[0]
    zero_state = jnp.zeros((bp, N_GROUPS, STATE_P), x_prompt.dtype)
    zero_buf = jnp.zeros((bp, CONV_W - 1, 2 * D_FF), x_prompt.dtype)
    xp, xs = x_prompt, x_sample
    kv_p, kv_s, re_p, im_p, re_s, im_s, cv_p, cv_s = [], [], [], [], [], [], [], []
    for l in range(DEPTH):
        xp, rows, hre, him, buf = layer(l, xp, c_prompt, pos_p, zero_state, zero_state,
                                        zero_buf, attend_prompt)
        kv_p.append(rows); re_p.append(hre); im_p.append(him); cv_p.append(buf)
        xs, rows, hre, him, buf = layer(l, xs, c_sample, pos_s, state_ssm_re[l], state_ssm_im[l],
                                        state_conv[l], attend_sample)
        kv_s.append(rows); re_s.append(hre); im_s.append(him); cv_s.append(buf)
    return (xp, xs, jnp.stack(kv_p), jnp.stack(kv_s), jnp.stack(re_p), jnp.stack(im_p),
            jnp.stack(re_s), jnp.stack(im_s), jnp.stack(cv_p), jnp.stack(cv_s))
```

```python
import functools
import math

import jax
import jax.numpy as jnp
from jax import lax
from jax.experimental import pallas as pl
from jax.experimental.pallas import tpu as pltpu

F32 = jnp.float32
BF16 = jnp.bfloat16

EPS = 1e-6
ROPE_THETA = 10000.0
PAGE = 128
NOPE = 128
ROPE = 64
HEAD_PAD = 256
GROUP_CHUNK = 16
VMEM_LIMIT = 56 * 1024 * 1024


def _cparams(*sem):
    return pltpu.CompilerParams(dimension_semantics=sem, vmem_limit_bytes=VMEM_LIMIT)


def _dot(a, b):
    return jnp.dot(a, b, preferred_element_type=F32)


def _dot_nt(a, b):
    return lax.dot_general(a, b, (((1,), (1,)), ((), ())), preferred_element_type=F32)


def _rms(x):
    return x * lax.rsqrt(jnp.mean(x * x, axis=-1, keepdims=True) + EPS)


def _ada_kernel(c_ref, w_ref, b_ref, o_ref):
    c = c_ref[...]
    a = (c * jax.nn.sigmoid(c)).astype(BF16)
    o_ref[...] = _dot(a, w_ref[...].astype(BF16)) + b_ref[...]


def ada_modulation(c_all, w_ada, b_ada):
    depth, d, n = w_ada.shape
    rows = c_all.shape[0]
    tn = 1024
    return pl.pallas_call(
        _ada_kernel,
        out_shape=jax.ShapeDtypeStruct((depth, rows, n), F32),
        grid=(depth, n // tn),
        in_specs=[pl.BlockSpec((rows, d), lambda l, j: (0, 0)),
                  pl.BlockSpec((None, d, tn), lambda l, j: (l, 0, j)),
                  pl.BlockSpec((None, 1, tn), lambda l, j: (l, 0, j))],
        out_specs=pl.BlockSpec((None, rows, tn), lambda l, j: (l, 0, j)),
        compiler_params=_cparams("arbitrary", "arbitrary"),
        name="ada_modulation",
    )(c_all, w_ada, b_ada.reshape(depth, 1, n))


class Stream:
    def __init__(self, rows, tm, mod, seq):
        self.rows, self.tm, self.mod, self.seq = rows, tm, mod, seq
        self.r = mod.shape[2]

    def mod_spec(self, l, chunk, d, row_tile_of):
        tm, seq, r = self.tm, self.seq, self.r
        if r == 1:
            return pl.BlockSpec((None, None, 1, d),
                                lambda *g: (l, (row_tile_of(*g) * tm) // seq, 0, chunk))
        return pl.BlockSpec((None, None, r, d), lambda *g: (l, 0, 0, chunk))


def _norm_mod_kernel(x_ref, g_ref, sh_ref, sc_ref, o_ref):
    h = _rms(x_ref[...]) * g_ref[...]
    o_ref[...] = (h * (1.0 + sc_ref[...]) + sh_ref[...]).astype(BF16)


def norm_mod(st, x, g, l, sh_chunk, sc_chunk):
    rows, d = x.shape
    tm = min(st.tm, 512)
    st2 = Stream(rows, tm, st.mod, st.seq)
    return pl.pallas_call(
        _norm_mod_kernel,
        out_shape=jax.ShapeDtypeStruct((rows, d), BF16),
        grid=(rows // tm,),
        in_specs=[pl.BlockSpec((tm, d), lambda i: (i, 0)),
                  pl.BlockSpec((None, 1, d), lambda i: (l, 0, 0)),
                  st2.mod_spec(l, sh_chunk, d, lambda i: i),
                  st2.mod_spec(l, sc_chunk, d, lambda i: i)],
        out_specs=pl.BlockSpec((tm, d), lambda i: (i, 0)),
        compiler_params=_cparams("arbitrary"),
        name="norm_mod",
    )(x, g, st.mod, st.mod)


def _mm_kernel(x_ref, w_ref, o_ref):
    o_ref[...] = _dot(x_ref[...], w_ref[...]).astype(o_ref.dtype)


def matmul(x, w, l, tm, tn, out_dtype=F32):
    rows, k = x.shape
    n = w.shape[-1]
    return pl.pallas_call(
        _mm_kernel,
        out_shape=jax.ShapeDtypeStruct((rows, n), out_dtype),
        grid=(n // tn, rows // tm),
        in_specs=[pl.BlockSpec((tm, k), lambda j, i: (i, 0)),
                  pl.BlockSpec((None, k, tn), lambda j, i: (l, 0, j))],
        out_specs=pl.BlockSpec((tm, tn), lambda j, i: (i, j)),
        compiler_params=_cparams("arbitrary", "arbitrary"),
        name="matmul",
    )(x, w)


def _ssm_param_kernel(lre_ref, lim_ref, ls_ref, bre_ref, bim_ref,
                      are_ref, aim_ref, bbre_ref, bbim_ref):
    lam_re, lam_im = lre_ref[...], lim_ref[...]
    dt = jnp.exp(ls_ref[...])
    decay = jnp.exp(lam_re * dt)
    a_re = decay * jnp.cos(lam_im * dt)
    a_im = decay * jnp.sin(lam_im * dt)
    den = lam_re * lam_re + lam_im * lam_im
    f_re = ((a_re - 1.0) * lam_re + a_im * lam_im) / den
    f_im = (a_im * lam_re - (a_re - 1.0) * lam_im) / den
    are_ref[...] = a_re
    aim_ref[...] = a_im
    b_re, b_im = bre_ref[...], bim_ref[...]
    bbre_ref[...] = f_re * b_re - f_im * b_im
    bbim_ref[...] = f_re * b_im + f_im * b_re


def ssm_params(lam_re, lam_im, log_step, b_re, b_im):
    depth, g, p = lam_re.shape
    c = b_re.shape[-1]
    rep = lambda a: jnp.broadcast_to(a[:, :, None, :], (depth, g, c, p)).reshape(depth, g * c, p)
    b_re_t = b_re.transpose(0, 1, 3, 2).reshape(depth, g * c, p)
    b_im_t = b_im.transpose(0, 1, 3, 2).reshape(depth, g * c, p)
    ls = jnp.broadcast_to(log_step[:, :, None, None], (depth, g, c, p)).reshape(depth, g * c, p)
    spec = pl.BlockSpec((None, g * c, p), lambda l: (l, 0, 0))
    a_re, a_im, bb_re, bb_im = pl.pallas_call(
        _ssm_param_kernel,
        out_shape=(jax.ShapeDtypeStruct((depth, g * c, p), F32),) * 4,
        grid=(depth,),
        in_specs=[spec] * 5,
        out_specs=(spec,) * 4,
        compiler_params=_cparams("arbitrary"),
        name="ssm_params",
    )(rep(lam_re), rep(lam_im), ls, b_re_t, b_im_t)
    first = lambda a: a.reshape(depth, g, c, p)[:, :, 0, :]
    return first(a_re), first(a_im), bb_re.reshape(depth, g, c, p), bb_im.reshape(depth, g, c, p)


def _block_diag_in(bb_t):
    depth, g, c, p = bb_t.shape
    k = g // GROUP_CHUNK
    eye = jnp.eye(GROUP_CHUNK, dtype=bb_t.dtype)
    x = bb_t.reshape(depth, k, GROUP_CHUNK, c, 1, p) * eye[None, None, :, None, :, None]
    return x.reshape(depth, k, GROUP_CHUNK * c, GROUP_CHUNK * p)


def _block_diag_out(c_w):
    depth, g, c, p = c_w.shape
    k = g // GROUP_CHUNK
    eye = jnp.eye(GROUP_CHUNK, dtype=c_w.dtype)
    x = c_w.reshape(depth, k, GROUP_CHUNK, c, p).transpose(0, 1, 2, 4, 3)
    x = x[:, :, :, :, None, :] * eye[None, None, :, None, :, None]
    return x.reshape(depth, k, GROUP_CHUNK * p, GROUP_CHUNK * c)


def _ssm_tail(y, u, dskip, wglu, bglu):
    y = y + dskip * u
    z = jax.nn.gelu(y)
    gate = jax.nn.sigmoid(_dot(z.astype(BF16), wglu) + bglu)
    return (z * gate).astype(BF16)


def _ssm_prompt_kernel(u_ref, bb_ref, ccre_ref, ccim_ref, are_ref, aim_ref, dskip_ref,
                       wglu_ref, bglu_ref, y_ref, hre_ref, him_ref,
                       sre, sim, cre, cim, *, tc, nchunk, ch_w, st_w):
    t = pl.program_id(1)

    @pl.when(t == 0)
    def _():
        cre[...] = jnp.zeros_like(cre)
        cim[...] = jnp.zeros_like(cim)

    u = u_ref[...]
    ub = u.astype(BF16)
    for k in range(nchunk):
        bu = _dot(ub[:, k * ch_w:(k + 1) * ch_w], bb_ref[k])
        sre[:, k * st_w:(k + 1) * st_w] = bu[:, :st_w]
        sim[:, k * st_w:(k + 1) * st_w] = bu[:, st_w:]

    for k in range(nchunk):
        cols = slice(k * st_w, (k + 1) * st_w)
        ar, ai = are_ref[:, cols], aim_ref[:, cols]

        def body(r, carry, cols=cols, ar=ar, ai=ai):
            hr, hi = carry
            row = pl.ds(r, 1)
            nr = ar * hr - ai * hi + sre[row, cols]
            ni = ar * hi + ai * hr + sim[row, cols]
            sre[row, cols] = nr
            sim[row, cols] = ni
            return nr, ni

        hr, hi = lax.fori_loop(0, tc, body, (cre[:, cols], cim[:, cols]), unroll=8)
        cre[:, cols] = hr
        cim[:, cols] = hi

    ys = []
    for k in range(nchunk):
        cols = slice(k * st_w, (k + 1) * st_w)
        ys.append(_dot(sre[:, cols].astype(BF16), ccre_ref[k])
                  - _dot(sim[:, cols].astype(BF16), ccim_ref[k]))
    y = jnp.concatenate(ys, axis=1)
    y_ref[...] = _ssm_tail(y, u, dskip_ref[...], wglu_ref[...], bglu_ref[...])
    hre_ref[...] = cre[...]
    him_ref[...] = cim[...]


def ssm_prompt(z, l, batch, seq, bb, ccre, ccim, a_re, a_im, d_skip, w_glu, b_glu):
    rows = z.shape[0]
    nchunk, ch_w, st2 = bb.shape[1:]
    st_w = st2 // 2
    d_ssm = nchunk * ch_w
    n_state = nchunk * st_w
    tc = 256
    nt = seq // tc
    kern = functools.partial(_ssm_prompt_kernel, tc=tc, nchunk=nchunk, ch_w=ch_w, st_w=st_w)
    vec = lambda n: pl.BlockSpec((None, 1, n), lambda b, t: (l, 0, 0))
    return pl.pallas_call(
        kern,
        out_shape=(jax.ShapeDtypeStruct((rows, d_ssm), BF16),
                   jax.ShapeDtypeStruct((batch, 1, n_state), F32),
                   jax.ShapeDtypeStruct((batch, 1, n_state), F32)),
        grid=(batch, nt),
        in_specs=[pl.BlockSpec((tc, d_ssm), lambda b, t: (b * nt + t, 0)),
                  pl.BlockSpec((None, nchunk, ch_w, st2), lambda b, t: (l, 0, 0, 0)),
                  pl.BlockSpec((None, nchunk, st_w, ch_w), lambda b, t: (l, 0, 0, 0)),
                  pl.BlockSpec((None, nchunk, st_w, ch_w), lambda b, t: (l, 0, 0, 0)),
                  vec(n_state), vec(n_state), vec(d_ssm),
                  pl.BlockSpec((None, d_ssm, d_ssm), lambda b, t: (l, 0, 0)),
                  vec(d_ssm)],
        out_specs=(pl.BlockSpec((tc, d_ssm), lambda b, t: (b * nt + t, 0)),
                   pl.BlockSpec((None, 1, n_state), lambda b, t: (b, 0, 0)),
                   pl.BlockSpec((None, 1, n_state), lambda b, t: (b, 0, 0))),
        scratch_shapes=[pltpu.VMEM((tc, n_state), F32), pltpu.VMEM((tc, n_state), F32),
                        pltpu.VMEM((1, n_state), F32), pltpu.VMEM((1, n_state), F32)],
        compiler_params=_cparams("arbitrary", "arbitrary"),
        name="ssm_prompt",
    )(z, bb, ccre, ccim, a_re, a_im, d_skip, w_glu, b_glu)


def _ssm_sample_kernel(u_ref, h0re_ref, h0im_ref, bbhi_ref, bblo_ref, ccre_ref, ccim_ref,
                       are_ref, aim_ref, dskip_ref, wglu_ref, bglu_ref,
                       y_ref, hre_ref, him_ref, *, nchunk, ch_w, st_w):
    u = u_ref[...]
    u_hi = u.astype(BF16)
    u_lo = (u - u_hi.astype(F32)).astype(BF16)
    ys = []
    for k in range(nchunk):
        cols = slice(k * st_w, (k + 1) * st_w)
        uh, ul = u_hi[:, k * ch_w:(k + 1) * ch_w], u_lo[:, k * ch_w:(k + 1) * ch_w]
        bu = _dot(uh, bbhi_ref[k]) + (_dot(uh, bblo_ref[k]) + _dot(ul, bbhi_ref[k]))
        ar, ai = are_ref[:, cols], aim_ref[:, cols]
        h0r, h0i = h0re_ref[:, cols], h0im_ref[:, cols]
        nr = ar * h0r - ai * h0i + bu[:, :st_w]
        ni = ar * h0i + ai * h0r + bu[:, st_w:]
        hre_ref[:, cols] = nr
        him_ref[:, cols] = ni
        ys.append(_dot(nr.astype(BF16), ccre_ref[k]) - _dot(ni.astype(BF16), ccim_ref[k]))
    y = jnp.concatenate(ys, axis=1)
    y_ref[...] = _ssm_tail(y, u, dskip_ref[...], wglu_ref[...], bglu_ref[...])


def ssm_sample(z, l, h0_re, h0_im, bb_hi, bb_lo, ccre, ccim, a_re, a_im, d_skip, w_glu, b_glu):
    rows = z.shape[0]
    nchunk, ch_w, st2 = bb_hi.shape[1:]
    st_w = st2 // 2
    d_ssm = nchunk * ch_w
    n_state = nchunk * st_w
    kern = functools.partial(_ssm_sample_kernel, nchunk=nchunk, ch_w=ch_w, st_w=st_w)
    vec = lambda n: pl.BlockSpec((None, 1, n), lambda i: (l, 0, 0))
    full = lambda a, b, c: pl.BlockSpec((None, a, b, c), lambda i: (l, 0, 0, 0))
    state = pl.BlockSpec((None, rows, n_state), lambda i: (l, 0, 0))
    return pl.pallas_call(
        kern,
        out_shape=(jax.ShapeDtypeStruct((rows, d_ssm), BF16),
                   jax.ShapeDtypeStruct((rows, n_state), F32),
                   jax.ShapeDtypeStruct((rows, n_state), F32)),
        grid=(1,),
        in_specs=[pl.BlockSpec((rows, d_ssm), lambda i: (0, 0)), state, state,
                  full(nchunk, ch_w, st2), full(nchunk, ch_w, st2),
                  full(nchunk, st_w, ch_w), full(nchunk, st_w, ch_w),
                  vec(n_state), vec(n_state), vec(d_ssm),
                  pl.BlockSpec((None, d_ssm, d_ssm), lambda i: (l, 0, 0)), vec(d_ssm)],
        out_specs=(pl.BlockSpec((rows, d_ssm), lambda i: (0, 0)),
                   pl.BlockSpec((rows, n_state), lambda i: (0, 0)),
                   pl.BlockSpec((rows, n_state), lambda i: (0, 0))),
        compiler_params=_cparams("arbitrary"),
        name="ssm_sample",
    )(z, h0_re, h0_im, bb_hi, bb_lo, ccre, ccim, a_re, a_im, d_skip, w_glu, b_glu)


def _q_kernel(ql_ref, g_ref, wa_ref, wb_ref, cos_ref, sin_ref, gq_ref, o_ref, *, heads, qk_dim):
    qn = (_rms(ql_ref[...]) * g_ref[...]).astype(BF16)
    qa = _dot(qn, wa_ref[...])
    qb = _dot(qn, wb_ref[...])
    cos, sin = cos_ref[...], sin_ref[...]
    g1, g2 = gq_ref[:, :NOPE], gq_ref[:, NOPE:]
    for h in range(heads):
        n = qa[:, h * HEAD_PAD:h * HEAD_PAD + NOPE]
        r = (qa[:, h * HEAD_PAD + NOPE:(h + 1) * HEAD_PAD] * cos
             + qb[:, h * NOPE:(h + 1) * NOPE] * sin)
        ssq = jnp.sum(n * n, axis=-1, keepdims=True) + jnp.sum(r * r, axis=-1, keepdims=True)
        inv = lax.rsqrt(ssq * (1.0 / qk_dim) + EPS)
        o_ref[:, h * HEAD_PAD:h * HEAD_PAD + NOPE] = (n * inv * g1).astype(BF16)
        o_ref[:, h * HEAD_PAD + NOPE:(h + 1) * HEAD_PAD] = (r * inv * g2).astype(BF16)


def mla_q(z, l, tm, q_col, g_q_lat, wqa, wqb, cos_t, sin_t, gq, heads, qk_dim):
    rows = z.shape[0]
    q_lora = wqa.shape[1]
    ntab = cos_t.shape[0] // tm
    kern = functools.partial(_q_kernel, heads=heads, qk_dim=qk_dim)
    return pl.pallas_call(
        kern,
        out_shape=jax.ShapeDtypeStruct((rows, heads * HEAD_PAD), BF16),
        grid=(rows // tm,),
        in_specs=[pl.BlockSpec((tm, q_lora), lambda i: (i, q_col)),
                  pl.BlockSpec((None, 1, q_lora), lambda i: (l, 0, 0)),
                  pl.BlockSpec((None, q_lora, heads * HEAD_PAD), lambda i: (l, 0, 0)),
                  pl.BlockSpec((None, q_lora, heads * NOPE), lambda i: (l, 0, 0)),
                  pl.BlockSpec((tm, NOPE), lambda i: (i % ntab, 0)),
                  pl.BlockSpec((tm, NOPE), lambda i: (i % ntab, 0)),
                  pl.BlockSpec((None, 1, HEAD_PAD), lambda i: (l, 0, 0))],
        out_specs=pl.BlockSpec((tm, heads * HEAD_PAD), lambda i: (i, 0)),
        compiler_params=_cparams("arbitrary"),
        name="mla_q",
    )(z, g_q_lat, wqa, wqb, cos_t, sin_t, gq)


def _kv_kernel(c_ref, kr_ref, g_ref, wk_ref, wv_ref, cos_ref, sin_ref, gk_ref,
               rows_ref, k_ref, v_ref, *, heads, qk_dim, kv_lora):
    cn = _rms(c_ref[...]) * g_ref[...]
    zkr = kr_ref[...]
    krr = zkr * cos_ref[...] + pltpu.roll(zkr, ROPE, axis=1) * sin_ref[...]
    rows_ref[:, :kv_lora] = cn
    rows_ref[:, kv_lora:] = krr[:, :ROPE]
    cb = cn.astype(BF16)
    kn = _dot(cb, wk_ref[...])
    v_ref[...] = _dot(cb, wv_ref[...]).astype(BF16)
    kr_sq = jnp.sum(krr * krr, axis=-1, keepdims=True)
    g1, g2 = gk_ref[:, :NOPE], gk_ref[:, NOPE:]
    krg = krr * g2
    for h in range(heads):
        n = kn[:, h * NOPE:(h + 1) * NOPE]
        inv = lax.rsqrt((jnp.sum(n * n, axis=-1, keepdims=True) + kr_sq) * (1.0 / qk_dim) + EPS)
        k_ref[:, h * HEAD_PAD:h * HEAD_PAD + NOPE] = (n * inv * g1).astype(BF16)
        k_ref[:, h * HEAD_PAD + NOPE:(h + 1) * HEAD_PAD] = (krg * inv).astype(BF16)


def mla_kv(z, zkr, l, tm, c_col, g_kv_lat, wk, wv, cos_t, sin_t, gk, heads, qk_dim):
    rows = z.shape[0]
    kv_lora = wk.shape[1]
    ntab = cos_t.shape[0] // tm
    kern = functools.partial(_kv_kernel, heads=heads, qk_dim=qk_dim, kv_lora=kv_lora)
    return pl.pallas_call(
        kern,
        out_shape=(jax.ShapeDtypeStruct((rows, kv_lora + ROPE), F32),
                   jax.ShapeDtypeStruct((rows, heads * HEAD_PAD), BF16),
                   jax.ShapeDtypeStruct((rows, heads * NOPE), BF16)),
        grid=(rows // tm,),
        in_specs=[pl.BlockSpec((tm, kv_lora), lambda i: (i, c_col)),
                  pl.BlockSpec((tm, 2 * ROPE), lambda i: (i, 0)),
                  pl.BlockSpec((None, 1, kv_lora), lambda i: (l, 0, 0)),
                  pl.BlockSpec((None, kv_lora, heads * NOPE), lambda i: (l, 0, 0)),
                  pl.BlockSpec((None, kv_lora, heads * NOPE), lambda i: (l, 0, 0)),
                  pl.BlockSpec((tm, NOPE), lambda i: (i % ntab, 0)),
                  pl.BlockSpec((tm, NOPE), lambda i: (i % ntab, 0)),
                  pl.BlockSpec((None, 1, HEAD_PAD), lambda i: (l, 0, 0))],
        out_specs=(pl.BlockSpec((tm, kv_lora + ROPE), lambda i: (i, 0)),
                   pl.BlockSpec((tm, heads * HEAD_PAD), lambda i: (i, 0)),
                   pl.BlockSpec((tm, heads * NOPE), lambda i: (i, 0))),
        compiler_params=_cparams("arbitrary"),
        name="mla_kv",
    )(z, zkr, g_kv_lat, wk, wv, cos_t, sin_t, gk)


def _flash_kernel(q_ref, k_ref, v_ref, o_ref, m_sc, l_sc, acc_sc, *, tq, tk, nk):
    i, j = pl.program_id(2), pl.program_id(3)

    @pl.when(j == 0)
    def _():
        m_sc[...] = jnp.full_like(m_sc, -jnp.inf)
        l_sc[...] = jnp.zeros_like(l_sc)
        acc_sc[...] = jnp.zeros_like(acc_sc)

    @pl.when(j <= i)
    def _():
        s = _dot_nt(q_ref[...], k_ref[...])
        qpos = i * tq + lax.broadcasted_iota(jnp.int32, (tq, tk), 0)
        kpos = j * tk + lax.broadcasted_iota(jnp.int32, (tq, tk), 1)
        s = jnp.where(kpos <= qpos, s, -jnp.inf)
        m_prev = m_sc[...]
        m_new = jnp.maximum(m_prev, jnp.max(s, axis=-1, keepdims=True))
        alpha = jnp.exp(m_prev - m_new)
        p = jnp.exp(s - m_new[:, :1])
        l_sc[...] = alpha * l_sc[...] + jnp.sum(p, axis=-1, keepdims=True)
        acc_sc[...] = alpha[:, :1] * acc_sc[...] + _dot(p.astype(BF16), v_ref[...])
        m_sc[...] = m_new

    @pl.when(j == nk - 1)
    def _():
        o_ref[...] = (acc_sc[...] / l_sc[:, :1]).astype(BF16)


def flash_prompt(q, k, v, batch, seq, heads):
    rows = q.shape[0]
    tq = tk = 512
    nq, nk = seq // tq, seq // tk
    kern = functools.partial(_flash_kernel, tq=tq, tk=tk, nk=nk)
    return pl.pallas_call(
        kern,
        out_shape=jax.ShapeDtypeStruct((rows, heads * NOPE), BF16),
        grid=(batch, heads, nq, nk),
        in_specs=[pl.BlockSpec((tq, HEAD_PAD), lambda b, h, i, j: (b * nq + i, h)),
                  pl.BlockSpec((tk, HEAD_PAD), lambda b, h, i, j: (b * nk + jnp.minimum(j, i), h)),
                  pl.BlockSpec((tk, NOPE), lambda b, h, i, j: (b * nk + jnp.minimum(j, i), h))],
        out_specs=pl.BlockSpec((tq, NOPE), lambda b, h, i, j: (b * nq + i, h)),
        scratch_shapes=[pltpu.VMEM((tq, NOPE), F32), pltpu.VMEM((tq, NOPE), F32),
                        pltpu.VMEM((tq, NOPE), F32)],
        compiler_params=_cparams("arbitrary", "arbitrary", "arbitrary", "arbitrary"),
        name="flash_prompt",
    )(q, k, v)


def _absorb_kernel(q_ref, k_ref, wk_ref, gk_ref, qabs_ref, qr_ref, sn_ref):
    q = q_ref[...].astype(F32)
    k = k_ref[...].astype(F32)
    sn = jnp.sum(q * k, axis=-1, keepdims=True)
    sn_ref[...] = jnp.broadcast_to(sn, sn_ref.shape)
    qg = q * gk_ref[...]
    qabs_ref[...] = _dot_nt(qg[:, :NOPE].astype(BF16), wk_ref[...]).astype(BF16)
    qr_ref[...] = qg[:, NOPE:].astype(BF16)


def mla_absorb(q, k_new, l, wk, gk, heads):
    rows = q.shape[0]
    kv_lora = wk.shape[1]
    return pl.pallas_call(
        _absorb_kernel,
        out_shape=(jax.ShapeDtypeStruct((heads, rows, kv_lora), BF16),
                   jax.ShapeDtypeStruct((heads, rows, NOPE), BF16),
                   jax.ShapeDtypeStruct((heads, rows, NOPE), F32)),
        grid=(heads,),
        in_specs=[pl.BlockSpec((rows, HEAD_PAD), lambda h: (0, h)),
                  pl.BlockSpec((rows, HEAD_PAD), lambda h: (0, h)),
                  pl.BlockSpec((None, kv_lora, NOPE), lambda h: (l, 0, h)),
                  pl.BlockSpec((None, 1, HEAD_PAD), lambda h: (l, 0, 0))],
        out_specs=(pl.BlockSpec((None, rows, kv_lora), lambda h: (h, 0, 0)),
                   pl.BlockSpec((None, rows, NOPE), lambda h: (h, 0, 0)),
                   pl.BlockSpec((None, rows, NOPE), lambda h: (h, 0, 0))),
        compiler_params=_cparams("arbitrary"),
        name="mla_absorb",
    )(q, k_new, wk, gk)


def _decode_kernel(pt_ref, r0_ref, r1_ref, wkt_ref, qabs_ref, qr_ref, sn_ref, cnew_ref,
                   o_ref, m_sc, l_sc, acc_sc, *, heads, kv_lora, qk_dim, nsteps):
    del pt_ref
    j = pl.program_id(1)

    @pl.when(j == 0)
    def _():
        m_sc[...] = sn_ref[...]
        l_sc[...] = jnp.ones_like(l_sc)
        acc_sc[...] = jnp.broadcast_to(cnew_ref[...], acc_sc.shape)

    rows = jnp.concatenate([r0_ref[...], r1_ref[...]], axis=0)
    nrow = rows.shape[0]
    c = rows[:, :kv_lora].astype(BF16)
    kr = rows[:, kv_lora:]
    kt = _dot_nt(wkt_ref[...], c)
    ssq = jnp.sum((kt * kt).reshape(heads, NOPE, nrow), axis=1)
    kr2 = kr * kr
    kr2_hi = kr2.astype(BF16)
    kr2_lo = (kr2 - kr2_hi.astype(F32)).astype(BF16)
    ones = jnp.ones((8, ROPE), BF16)
    kr_sq = (_dot_nt(ones, kr2_hi) + _dot_nt(ones, kr2_lo))[:1]
    inv = lax.rsqrt((ssq + kr_sq) * (1.0 / qk_dim) + EPS)
    s = (_dot_nt(qabs_ref[...], c) + _dot_nt(qr_ref[...], kr.astype(BF16))) * inv

    m_prev = m_sc[...]
    m_new = jnp.maximum(m_prev, jnp.max(s, axis=-1, keepdims=True))
    alpha = jnp.exp(m_prev - m_new)
    p = jnp.exp(s - m_new[:, :1])
    l_sc[...] = alpha * l_sc[...] + jnp.sum(p, axis=-1, keepdims=True)
    acc_sc[...] = alpha[:, :1] * acc_sc[...] + _dot(p.astype(BF16), c)
    m_sc[...] = m_new

    @pl.when(j == nsteps - 1)
    def _():
        o_ref[...] = acc_sc[...] / l_sc[:, :1]


def mla_decode(cache_kv, page_table, l, wkt, qabs, qr, sn, c_new, heads, qk_dim):
    bd, npages = page_table.shape
    kv_row = cache_kv.shape[-1]
    kv_lora = kv_row - ROPE
    nsteps = npages // 2
    kern = functools.partial(_decode_kernel, heads=heads, kv_lora=kv_lora, qk_dim=qk_dim,
                             nsteps=nsteps)
    grid_spec = pltpu.PrefetchScalarGridSpec(
        num_scalar_prefetch=1,
        grid=(bd, nsteps),
        in_specs=[pl.BlockSpec((None, None, PAGE, kv_row), lambda b, j, pt: (l, pt[b, 2 * j], 0, 0)),
                  pl.BlockSpec((None, None, PAGE, kv_row), lambda b, j, pt: (l, pt[b, 2 * j + 1], 0, 0)),
                  pl.BlockSpec((None, heads * NOPE, kv_lora), lambda b, j, pt: (l, 0, 0)),
                  pl.BlockSpec((None, heads, kv_lora), lambda b, j, pt: (b, 0, 0)),
                  pl.BlockSpec((None, heads, ROPE), lambda b, j, pt: (b, 0, 0)),
                  pl.BlockSpec((None, heads, NOPE), lambda b, j, pt: (b, 0, 0)),
                  pl.BlockSpec((None, 1, kv_lora), lambda b, j, pt: (b, 0, 0))],
        out_specs=pl.BlockSpec((None, heads, kv_lora), lambda b, j, pt: (b, 0, 0)),
        scratch_shapes=[pltpu.VMEM((heads, NOPE), F32), pltpu.VMEM((heads, NOPE), F32),
                        pltpu.VMEM((heads, kv_lora), F32)],
    )
    return pl.pallas_call(
        kern,
        out_shape=jax.ShapeDtypeStruct((bd, heads, kv_lora), F32),
        grid_spec=grid_spec,
        compiler_params=_cparams("arbitrary", "arbitrary"),
        name="mla_decode",
    )(page_table, cache_kv, cache_kv, wkt, qabs, qr, sn, c_new)


def _vup_kernel(o_ref, wv_ref, out_ref):
    out_ref[...] = _dot(o_ref[...].astype(BF16), wv_ref[...]).astype(BF16)


def mla_vup(o_lat_t, l, wv):
    heads, rows, kv_lora = o_lat_t.shape
    return pl.pallas_call(
        _vup_kernel,
        out_shape=jax.ShapeDtypeStruct((rows, heads * NOPE), BF16),
        grid=(heads,),
        in_specs=[pl.BlockSpec((None, rows, kv_lora), lambda h: (h, 0, 0)),
                  pl.BlockSpec((None, kv_lora, NOPE), lambda h: (l, 0, h))],
        out_specs=pl.BlockSpec((rows, NOPE), lambda h: (0, h)),
        compiler_params=_cparams("arbitrary"),
        name="mla_vup",
    )(o_lat_t, wv)


def _merge_kernel(y_ref, o_ref, ga_ref, gb_ref, ws_ref, wa_ref, out_ref):
    a = _dot(y_ref[...], ws_ref[...])
    b = _dot(o_ref[...], wa_ref[...])
    out_ref[...] = (jax.nn.sigmoid(ga_ref[...]) * a + jax.nn.sigmoid(gb_ref[...]) * b).astype(BF16)


def merge(y, o, z, l, tm, tn, ga_col0, gb_col0, w_ssm_out, w_att_out):
    rows, d_ssm = y.shape
    d_att = o.shape[1]
    d = w_ssm_out.shape[-1]
    ga0, gb0 = ga_col0 // tn, gb_col0 // tn
    return pl.pallas_call(
        _merge_kernel,
        out_shape=jax.ShapeDtypeStruct((rows, d), BF16),
        grid=(d // tn, rows // tm),
        in_specs=[pl.BlockSpec((tm, d_ssm), lambda j, i: (i, 0)),
                  pl.BlockSpec((tm, d_att), lambda j, i: (i, 0)),
                  pl.BlockSpec((tm, tn), lambda j, i: (i, ga0 + j)),
                  pl.BlockSpec((tm, tn), lambda j, i: (i, gb0 + j)),
                  pl.BlockSpec((None, d_ssm, tn), lambda j, i: (l, 0, j)),
                  pl.BlockSpec((None, d_att, tn), lambda j, i: (l, 0, j))],
        out_specs=pl.BlockSpec((tm, tn), lambda j, i: (i, j)),
        compiler_params=_cparams("arbitrary", "arbitrary"),
        name="merge",
    )(y, o, z, z, w_ssm_out, w_att_out)


def _oproj_kernel(m_ref, w_ref, x_ref, gt_ref, g_ref, sh_ref, sc_ref, x1_ref, h_ref):
    x1 = x_ref[...] + gt_ref[...] * _dot(m_ref[...], w_ref[...])
    x1_ref[...] = x1
    h = _rms(x1) * g_ref[...]
    h_ref[...] = (h * (1.0 + sc_ref[...]) + sh_ref[...]).astype(BF16)


def oproj(st, merged, x, l, w_o, g_ffn):
    rows, d = x.shape
    tm = min(st.tm, 512)
    st2 = Stream(rows, tm, st.mod, st.seq)
    ident = lambda i: i
    return pl.pallas_call(
        _oproj_kernel,
        out_shape=(jax.ShapeDtypeStruct((rows, d), F32), jax.ShapeDtypeStruct((rows, d), BF16)),
        grid=(rows // tm,),
        in_specs=[pl.BlockSpec((tm, d), lambda i: (i, 0)),
                  pl.BlockSpec((None, d, d), lambda i: (l, 0, 0)),
                  pl.BlockSpec((tm, d), lambda i: (i, 0)),
                  st2.mod_spec(l, 2, d, ident),
                  pl.BlockSpec((None, 1, d), lambda i: (l, 0, 0)),
                  st2.mod_spec(l, 3, d, ident),
                  st2.mod_spec(l, 4, d, ident)],
        out_specs=(pl.BlockSpec((tm, d), lambda i: (i, 0)), pl.BlockSpec((tm, d), lambda i: (i, 0))),
        compiler_params=_cparams("arbitrary"),
        name="oproj",
    )(merged, w_o, x, st.mod, g_ffn, st.mod, st.mod)


def _conv3(s_ref, cw_ref, cb_ref, tm):
    return (cb_ref[...] + cw_ref[0:1, :] * s_ref[pl.ds(6, tm), :]
            + cw_ref[1:2, :] * s_ref[pl.ds(7, tm), :] + cw_ref[2:3, :] * s_ref[pl.ds(8, tm), :])


def _ffn_up_prompt_kernel(h_ref, wg_ref, wv_ref, cwg_ref, cwv_ref, cbg_ref, cbv_ref,
                          act_ref, tg_ref, tv_ref, sg, sv, *, tm, tiles_per_batch):
    i = pl.program_id(1)

    @pl.when(i % tiles_per_batch == 0)
    def _():
        sg[0:8, :] = jnp.zeros((8, sg.shape[1]), F32)
        sv[0:8, :] = jnp.zeros((8, sv.shape[1]), F32)

    h = h_ref[...]
    sg[pl.ds(8, tm), :] = _dot(h, wg_ref[...])
    sv[pl.ds(8, tm), :] = _dot(h, wv_ref[...])
    zg = _conv3(sg, cwg_ref, cbg_ref, tm)
    zv = _conv3(sv, cwv_ref, cbv_ref, tm)
    act_ref[...] = (zg * jax.nn.sigmoid(zg) * zv).astype(BF16)
    tail_g = sg[pl.ds(tm, 8), :]
    tail_v = sv[pl.ds(tm, 8), :]
    tg_ref[...] = tail_g
    tv_ref[...] = tail_v
    sg[0:8, :] = tail_g
    sv[0:8, :] = tail_v


def ffn_up_prompt(h, l, batch, seq, w_up, conv_w, conv_b):
    rows, d = h.shape
    d_ff = w_up.shape[-1] // 2
    tm, tn = 1024, 512
    nj = d_ff // tn
    tpb = seq // tm
    kern = functools.partial(_ffn_up_prompt_kernel, tm=tm, tiles_per_batch=tpb)
    return pl.pallas_call(
        kern,
        out_shape=(jax.ShapeDtypeStruct((rows, d_ff), BF16),
                   jax.ShapeDtypeStruct((batch, 8, d_ff), F32),
                   jax.ShapeDtypeStruct((batch, 8, d_ff), F32)),
        grid=(nj, rows // tm),
        in_specs=[pl.BlockSpec((tm, d), lambda j, i: (i, 0)),
                  pl.BlockSpec((None, d, tn), lambda j, i: (l, 0, j)),
                  pl.BlockSpec((None, d, tn), lambda j, i: (l, 0, nj + j)),
                  pl.BlockSpec((None, 3, tn), lambda j, i: (l, 0, j)),
                  pl.BlockSpec((None, 3, tn), lambda j, i: (l, 0, nj + j)),
                  pl.BlockSpec((None, 1, tn), lambda j, i: (l, 0, j)),
                  pl.BlockSpec((None, 1, tn), lambda j, i: (l, 0, nj + j))],
        out_specs=(pl.BlockSpec((tm, tn), lambda j, i: (i, j)),
                   pl.BlockSpec((None, 8, tn), lambda j, i: (i // tpb, 0, j)),
                   pl.BlockSpec((None, 8, tn), lambda j, i: (i // tpb, 0, j))),
        scratch_shapes=[pltpu.VMEM((tm + 8, tn), F32), pltpu.VMEM((tm + 8, tn), F32)],
        compiler_params=_cparams("arbitrary", "arbitrary"),
        name="ffn_up_prompt",
    )(h, w_up, w_up, conv_w, conv_w, conv_b, conv_b)


def _ffn_up_sample_kernel(h_ref, wg_ref, wv_ref, cwg_ref, cwv_ref, cbg_ref, cbv_ref,
                          g0_ref, g1_ref, v0_ref, v1_ref, act_ref, ug_ref, uv_ref):
    h = h_ref[...]
    ug = _dot(h, wg_ref[...])
    uv = _dot(h, wv_ref[...])
    ug_ref[...] = ug
    uv_ref[...] = uv
    zg = (cbg_ref[...] + g0_ref[...] * cwg_ref[0:1, :] + g1_ref[...] * cwg_ref[1:2, :]
          + ug * cwg_ref[2:3, :])
    zv = (cbv_ref[...] + v0_ref[...] * cwv_ref[0:1, :] + v1_ref[...] * cwv_ref[1:2, :]
          + uv * cwv_ref[2:3, :])
    act_ref[...] = (zg * jax.nn.sigmoid(zg) * zv).astype(BF16)


def ffn_up_sample(h, l, w_up, conv_w, conv_b, buf0, buf1):
    rows, d = h.shape
    d_ff = w_up.shape[-1] // 2
    tn = 512
    nj = d_ff // tn
    col = lambda off: pl.BlockSpec((rows, tn), lambda j: (0, off + j))
    return pl.pallas_call(
        _ffn_up_sample_kernel,
        out_shape=(jax.ShapeDtypeStruct((rows, d_ff), BF16),
                   jax.ShapeDtypeStruct((rows, d_ff), F32),
                   jax.ShapeDtypeStruct((rows, d_ff), F32)),
        grid=(nj,),
        in_specs=[pl.BlockSpec((rows, d), lambda j: (0, 0)),
                  pl.BlockSpec((None, d, tn), lambda j: (l, 0, j)),
                  pl.BlockSpec((None, d, tn), lambda j: (l, 0, nj + j)),
                  pl.BlockSpec((None, 3, tn), lambda j: (l, 0, j)),
                  pl.BlockSpec((None, 3, tn), lambda j: (l, 0, nj + j)),
                  pl.BlockSpec((None, 1, tn), lambda j: (l, 0, j)),
                  pl.BlockSpec((None, 1, tn), lambda j: (l, 0, nj + j)),
                  col(0), col(0), col(nj), col(nj)],
        out_specs=(col(0), col(0), col(0)),
        compiler_params=_cparams("arbitrary"),
        name="ffn_up_sample",
    )(h, w_up, w_up, conv_w, conv_w, conv_b, conv_b, buf0, buf1, buf0, buf1)


def _ffn_down_kernel(a_ref, w_ref, x_ref, gt_ref, o_ref):
    o_ref[...] = x_ref[...] + gt_ref[...] * _dot(a_ref[...], w_ref[...])


def ffn_down(st, act, x1, l, w_down):
    rows, d_ff = act.shape
    d = x1.shape[1]
    tm = min(st.tm, 512)
    tn = 512
    st2 = Stream(rows, tm, st.mod, st.seq)
    ncol = d // tn
    if st2.r == 1:
        gt_spec = pl.BlockSpec((None, None, 1, tn),
                               lambda j, i: (l, (i * tm) // st2.seq, 0, 5 * ncol + j))
    else:
        gt_spec = pl.BlockSpec((None, None, st2.r, tn), lambda j, i: (l, 0, 0, 5 * ncol + j))
    return pl.pallas_call(
        _ffn_down_kernel,
        out_shape=jax.ShapeDtypeStruct((rows, d), F32),
        grid=(ncol, rows // tm),
        in_specs=[pl.BlockSpec((tm, d_ff), lambda j, i: (i, 0)),
                  pl.BlockSpec((None, d_ff, tn), lambda j, i: (l, 0, j)),
                  pl.BlockSpec((tm, tn), lambda j, i: (i, j)),
                  gt_spec],
        out_specs=pl.BlockSpec((tm, tn), lambda j, i: (i, j)),
        compiler_params=_cparams("arbitrary", "arbitrary"),
        name="ffn_down",
    )(act, w_down, x1, st.mod)


def _rot_cols(w):
    half = w.shape[-1] // 2
    return jnp.concatenate([-w[..., half:], w[..., :half]], axis=-1)


def _rope_tables(pos, rows):
    half = ROPE // 2
    inv = ROPE_THETA ** (-jnp.arange(half, dtype=F32) / half)
    ang = pos.astype(F32)[:, None] * inv[None, :]
    cos, sin = jnp.cos(ang), jnp.sin(ang)
    pad = jnp.zeros((pos.shape[0], NOPE - ROPE), F32)
    cos_t = jnp.concatenate([cos, cos, pad], axis=-1)
    sin_t = jnp.concatenate([sin, sin, pad], axis=-1)
    if cos_t.shape[0] != rows:
        cos_t = jnp.broadcast_to(cos_t, (rows, NOPE))
        sin_t = jnp.broadcast_to(sin_t, (rows, NOPE))
    return cos_t, sin_t


def kernel(x_prompt, x_sample, c_prompt, c_sample, cache_kv, state_ssm_re, state_ssm_im, state_conv, page_table, w_ada, b_ada, g_norm_mix, g_norm_ffn, w_in, lam_re, lam_im, log_step, b_re, b_im, c_re, c_im, d_skip, w_glu, b_glu, w_ssm_out, g_q_lat, w_uq, g_kv_lat, w_uk, w_uv, g_qk_q, g_qk_k, w_att_out, w_o, w_up, conv_w, conv_b, w_down):
    batch, seq, d = x_prompt.shape
    bd = x_sample.shape[0]
    depth = w_in.shape[0]
    d_ssm = d_skip.shape[-1]
    q_lora = w_uq.shape[1]
    kv_lora, heads, nope = w_uk.shape[1:]
    qk_dim = g_qk_q.shape[-1]
    rope = qk_dim - nope
    n_groups, state_p = lam_re.shape[1:]
    d_ff = w_down.shape[1]
    assert (nope, rope) == (NOPE, ROPE) and cache_kv.shape[2] == PAGE and x_sample.shape[1] == 1
    past_len = page_table.shape[1] * PAGE
    scale = qk_dim ** -0.5

    c_all = jnp.concatenate([c_sample, c_prompt, jnp.zeros((8 - batch % 8, d), F32)], axis=0)
    mod = ada_modulation(c_all, w_ada, b_ada)
    st_p = Stream(batch * seq, 1024, mod[:, bd:bd + batch].reshape(depth, batch, 1, 6 * d), seq)
    st_s = Stream(bd, bd, mod[:, :bd].reshape(depth, 1, bd, 6 * d), 1)

    vec3 = lambda a: a.reshape(depth, 1, a.shape[-1])
    s0, s1, s2, s3, s4 = d_ssm, d_ssm + q_lora, d_ssm + q_lora + kv_lora, \
        d_ssm + q_lora + kv_lora + rope, d_ssm + q_lora + kv_lora + rope + d
    w_main = jnp.concatenate([w_in[..., :s2], w_in[..., s3:]], axis=-1).astype(BF16)
    w_kr = w_in[..., s2:s3]
    w_kr = jnp.concatenate([w_kr, _rot_cols(w_kr)], axis=-1).astype(BF16)
    q_col, c_col = s0 // q_lora, s1 // kv_lora
    ga_col0, gb_col0 = s2, s2 + d

    wq = w_uq.reshape(depth, q_lora, heads, qk_dim)
    wq_n, wq_r = wq[..., :nope], wq[..., nope:]
    zpad = jnp.zeros((depth, q_lora, heads, HEAD_PAD - qk_dim), F32)
    wqa = jnp.concatenate([wq_n, wq_r, zpad], axis=-1).reshape(depth, q_lora, heads * HEAD_PAD).astype(BF16)
    wqb = jnp.concatenate([_rot_cols(wq_r), zpad], axis=-1).reshape(depth, q_lora, heads * NOPE).astype(BF16)
    wk2 = w_uk.reshape(depth, kv_lora, heads * nope).astype(BF16)
    wv2 = w_uv.reshape(depth, kv_lora, heads * nope).astype(BF16)
    wkt = wk2.transpose(0, 2, 1)
    gpad = jnp.zeros((depth, HEAD_PAD - qk_dim), F32)
    gq = vec3(jnp.concatenate([g_qk_q, gpad], axis=-1) * scale)
    gk = vec3(jnp.concatenate([g_qk_k, gpad], axis=-1))

    a_re, a_im, bbt_re, bbt_im = ssm_params(lam_re, lam_im, log_step, b_re, b_im)
    bb = jnp.concatenate([_block_diag_in(bbt_re), _block_diag_in(bbt_im)], axis=-1)
    bb_hi = bb.astype(BF16)
    bb_lo = (bb - bb_hi.astype(F32)).astype(BF16)
    ccre = _block_diag_out(c_re).astype(BF16)
    ccim = _block_diag_out(c_im).astype(BF16)
    n_state = n_groups * state_p
    a_re3, a_im3 = a_re.reshape(depth, 1, n_state), a_im.reshape(depth, 1, n_state)

    w_glu_b, w_sso_b, w_ao_b = w_glu.astype(BF16), w_ssm_out.astype(BF16), w_att_out.astype(BF16)
    w_o_b, w_up_b, w_down_b = w_o.astype(BF16), w_up.astype(BF16), w_down.astype(BF16)
    g_mix3, g_ffn3, g_ql3, g_kv3 = vec3(g_norm_mix), vec3(g_norm_ffn), vec3(g_q_lat), vec3(g_kv_lat)
    d_skip3, b_glu3, conv_b3 = vec3(d_skip), vec3(b_glu), vec3(conv_b)

    cos_p, sin_p = _rope_tables(jnp.arange(seq), seq)
    cos_s, sin_s = _rope_tables(jnp.full((1,), past_len), bd)

    xp = x_prompt.reshape(batch * seq, d)
    xs = x_sample.reshape(bd, d)
    h0_re = state_ssm_re.reshape(depth, bd, n_state)
    h0_im = state_ssm_im.reshape(depth, bd, n_state)

    outs = {k: [] for k in ("kv_p", "kv_s", "re_p", "im_p", "re_s", "im_s", "cv_p", "cv_s")}
    for l in range(depth):
        h = norm_mod(st_p, xp, g_mix3, l, 0, 1)
        z = matmul(h, w_main, l, 1024, 1024)
        zkr = matmul(h, w_kr, l, 1024, 2 * ROPE)
        y_ssm, hre, him = ssm_prompt(z, l, batch, seq, bb_hi, ccre, ccim, a_re3, a_im3,
                                     d_skip3, w_glu_b, b_glu3)
        q = mla_q(z, l, 512, q_col, g_ql3, wqa, wqb, cos_p, sin_p, gq, heads, qk_dim)
        rows_p, k, v = mla_kv(z, zkr, l, 512, c_col, g_kv3, wk2, wv2, cos_p, sin_p, gk, heads, qk_dim)
        o = flash_prompt(q, k, v, batch, seq, heads)
        merged = merge(y_ssm, o, z, l, 1024, 512, ga_col0, gb_col0, w_sso_b, w_ao_b)
        x1, h2 = oproj(st_p, merged, xp, l, w_o_b, g_ffn3)
        act, tail_g, tail_v = ffn_up_prompt(h2, l, batch, seq, w_up_b, conv_w, conv_b3)
        xp = ffn_down(st_p, act, x1, l, w_down_b)
        outs["kv_p"].append(rows_p.reshape(batch, seq, kv_lora + rope))
        outs["re_p"].append(hre.reshape(batch, n_groups, state_p))
        outs["im_p"].append(him.reshape(batch, n_groups, state_p))
        outs["cv_p"].append(jnp.concatenate([tail_g[:, 6:], tail_v[:, 6:]], axis=-1))

        h = norm_mod(st_s, xs, g_mix3, l, 0, 1)
        z = matmul(h, w_main, l, bd, 1024)
        zkr = matmul(h, w_kr, l, bd, 2 * ROPE)
        y_ssm, hre, him = ssm_sample(z, l, h0_re, h0_im, bb_hi, bb_lo, ccre, ccim, a_re3, a_im3,
                                     d_skip3, w_glu_b, b_glu3)
        q = mla_q(z, l, bd, q_col, g_ql3, wqa, wqb, cos_s, sin_s, gq, heads, qk_dim)
        rows_s, k_new, _ = mla_kv(z, zkr, l, bd, c_col, g_kv3, wk2, wv2, cos_s, sin_s, gk, heads, qk_dim)
        qabs, qr, sn = mla_absorb(q, k_new, l, wk2, gk, heads)
        o_lat = mla_decode(cache_kv, page_table, l, wkt,
                           qabs.transpose(1, 0, 2), qr[:, :, :ROPE].transpose(1, 0, 2),
                           sn.transpose(1, 0, 2), rows_s[:, None, :kv_lora], heads, qk_dim)
        o = mla_vup(o_lat.transpose(1, 0, 2), l, wv2)
        merged = merge(y_ssm, o, z, l, bd, 512, ga_col0, gb_col0, w_sso_b, w_ao_b)
        x1, h2 = oproj(st_s, merged, xs, l, w_o_b, g_ffn3)
        buf0, buf1 = state_conv[l, :, 0, :], state_conv[l, :, 1, :]
        act, ug, uv = ffn_up_sample(h2, l, w_up_b, conv_w, conv_b3, buf0, buf1)
        xs = ffn_down(st_s, act, x1, l, w_down_b)
        outs["kv_s"].append(rows_s.reshape(bd, 1, kv_lora + rope))
        outs["re_s"].append(hre.reshape(bd, n_groups, state_p))
        outs["im_s"].append(him.reshape(bd, n_groups, state_p))
        outs["cv_s"].append(jnp.stack([buf1, jnp.concatenate([ug, uv], axis=-1)], axis=1))

    st = lambda key: jnp.stack(outs[key])
    return (xp.reshape(batch, seq, d), xs.reshape(bd, 1, d), st("kv_p"), st("kv_s"),
            st("re_p"), st("im_p"), st("re_s"), st("im_s"), st("cv_p"), st("cv_s"))
```

```python
import functools
import math

import jax
import jax.numpy as jnp
from jax import lax
from jax.experimental import pallas as pl
from jax.experimental.pallas import tpu as pltpu

F32 = jnp.float32
BF16 = jnp.bfloat16

EPS = 1e-6
ROPE_THETA = 10000.0
PAGE = 128
NOPE = 128
ROPE = 64
HEAD_PAD = 256
GROUP_CHUNK = 16
VMEM_LIMIT = 56 * 1024 * 1024


def _cparams(*sem):
    return pltpu.CompilerParams(dimension_semantics=sem, vmem_limit_bytes=VMEM_LIMIT)


def _dot(a, b):
    return jnp.dot(a, b, preferred_element_type=F32)


def _dot_nt(a, b):
    return lax.dot_general(a, b, (((1,), (1,)), ((), ())), preferred_element_type=F32)


def _rms(x):
    return x * lax.rsqrt(jnp.mean(x * x, axis=-1, keepdims=True) + EPS)


def _ada_kernel(c_ref, w_ref, b_ref, o_ref):
    c = c_ref[...]
    a = (c * jax.nn.sigmoid(c)).astype(BF16)
    o_ref[...] = _dot(a, w_ref[...].astype(BF16)) + b_ref[...]


def ada_modulation(c_all, w_ada, b_ada):
    depth, d, n = w_ada.shape
    rows = c_all.shape[0]
    tn = 1024
    return pl.pallas_call(
        _ada_kernel,
        out_shape=jax.ShapeDtypeStruct((depth, rows, n), F32),
        grid=(depth, n // tn),
        in_specs=[pl.BlockSpec((rows, d), lambda l, j: (0, 0)),
                  pl.BlockSpec((None, d, tn), lambda l, j: (l, 0, j)),
                  pl.BlockSpec((None, 1, tn), lambda l, j: (l, 0, j))],
        out_specs=pl.BlockSpec((None, rows, tn), lambda l, j: (l, 0, j)),
        compiler_params=_cparams("arbitrary", "arbitrary"),
        name="ada_modulation",
    )(c_all, w_ada, b_ada.reshape(depth, 1, n))


class Stream:
    def __init__(self, rows, tm, mod, seq):
        self.rows, self.tm, self.mod, self.seq = rows, tm, mod, seq
        self.r = mod.shape[2]

    def mod_spec(self, l, chunk, d, row_tile_of):
        tm, seq, r = self.tm, self.seq, self.r
        if r == 1:
            return pl.BlockSpec((None, None, 1, d),
                                lambda *g: (l, (row_tile_of(*g) * tm) // seq, 0, chunk))
        return pl.BlockSpec((None, None, r, d), lambda *g: (l, 0, 0, chunk))


def _norm_mod_kernel(x_ref, g_ref, sh_ref, sc_ref, o_ref):
    h = _rms(x_ref[...]) * g_ref[...]
    o_ref[...] = (h * (1.0 + sc_ref[...]) + sh_ref[...]).astype(BF16)


def norm_mod(st, x, g, l, sh_chunk, sc_chunk):
    rows, d = x.shape
    tm = min(st.tm, 512)
    st2 = Stream(rows, tm, st.mod, st.seq)
    return pl.pallas_call(
        _norm_mod_kernel,
        out_shape=jax.ShapeDtypeStruct((rows, d), BF16),
        grid=(rows // tm,),
        in_specs=[pl.BlockSpec((tm, d), lambda i: (i, 0)),
                  pl.BlockSpec((None, 1, d), lambda i: (l, 0, 0)),
                  st2.mod_spec(l, sh_chunk, d, lambda i: i),
                  st2.mod_spec(l, sc_chunk, d, lambda i: i)],
        out_specs=pl.BlockSpec((tm, d), lambda i: (i, 0)),
        compiler_params=_cparams("arbitrary"),
        name="norm_mod",
    )(x, g, st.mod, st.mod)


def _mm_kernel(x_ref, w_ref, o_ref):
    o_ref[...] = _dot(x_ref[...], w_ref[...]).astype(o_ref.dtype)


def matmul(x, w, l, tm, tn, out_dtype=F32):
    rows, k = x.shape
    n = w.shape[-1]
    return pl.pallas_call(
        _mm_kernel,
        out_shape=jax.ShapeDtypeStruct((rows, n), out_dtype),
        grid=(n // tn, rows // tm),
        in_specs=[pl.BlockSpec((tm, k), lambda j, i: (i, 0)),
                  pl.BlockSpec((None, k, tn), lambda j, i: (l, 0, j))],
        out_specs=pl.BlockSpec((tm, tn), lambda j, i: (i, j)),
        compiler_params=_cparams("arbitrary", "arbitrary"),
        name="matmul",
    )(x, w)


def _ssm_param_kernel(lre_ref, lim_ref, ls_ref, bre_ref, bim_ref,
                      are_ref, aim_ref, bbre_ref, bbim_ref):
    lam_re, lam_im = lre_ref[...], lim_ref[...]
    dt = jnp.exp(ls_ref[...])
    decay = jnp.exp(lam_re * dt)
    a_re = decay * jnp.cos(lam_im * dt)
    a_im = decay * jnp.sin(lam_im * dt)
    den = lam_re * lam_re + lam_im * lam_im
    f_re = ((a_re - 1.0) * lam_re + a_im * lam_im) / den
    f_im = (a_im * lam_re - (a_re - 1.0) * lam_im) / den
    are_ref[...] = a_re
    aim_ref[...] = a_im
    b_re, b_im = bre_ref[...], bim_ref[...]
    bbre_ref[...] = f_re * b_re - f_im * b_im
    bbim_ref[...] = f_re * b_im + f_im * b_re


def ssm_params(lam_re, lam_im, log_step, b_re, b_im):
    depth, g, p = lam_re.shape
    c = b_re.shape[-1]
    rep = lambda a: jnp.broadcast_to(a[:, :, None, :], (depth, g, c, p)).reshape(depth, g * c, p)
    b_re_t = b_re.transpose(0, 1, 3, 2).reshape(depth, g * c, p)
    b_im_t = b_im.transpose(0, 1, 3, 2).reshape(depth, g * c, p)
    ls = jnp.broadcast_to(log_step[:, :, None, None], (depth, g, c, p)).reshape(depth, g * c, p)
    spec = pl.BlockSpec((None, g * c, p), lambda l: (l, 0, 0))
    a_re, a_im, bb_re, bb_im = pl.pallas_call(
        _ssm_param_kernel,
        out_shape=(jax.ShapeDtypeStruct((depth, g * c, p), F32),) * 4,
        grid=(depth,),
        in_specs=[spec] * 5,
        out_specs=(spec,) * 4,
        compiler_params=_cparams("arbitrary"),
        name="ssm_params",
    )(rep(lam_re), rep(lam_im), ls, b_re_t, b_im_t)
    first = lambda a: a.reshape(depth, g, c, p)[:, :, 0, :]
    return first(a_re), first(a_im), bb_re.reshape(depth, g, c, p), bb_im.reshape(depth, g, c, p)


def _block_diag_in(bb_t):
    depth, g, c, p = bb_t.shape
    k = g // GROUP_CHUNK
    eye = jnp.eye(GROUP_CHUNK, dtype=bb_t.dtype)
    x = bb_t.reshape(depth, k, GROUP_CHUNK, c, 1, p) * eye[None, None, :, None, :, None]
    return x.reshape(depth, k, GROUP_CHUNK * c, GROUP_CHUNK * p)


def _block_diag_out(c_w):
    depth, g, c, p = c_w.shape
    k = g // GROUP_CHUNK
    eye = jnp.eye(GROUP_CHUNK, dtype=c_w.dtype)
    x = c_w.reshape(depth, k, GROUP_CHUNK, c, p).transpose(0, 1, 2, 4, 3)
    x = x[:, :, :, :, None, :] * eye[None, None, :, None, :, None]
    return x.reshape(depth, k, GROUP_CHUNK * p, GROUP_CHUNK * c)


def _ssm_tail(y, u, dskip, wglu, bglu):
    y = y + dskip * u
    z = jax.nn.gelu(y)
    gate = jax.nn.sigmoid(_dot(z.astype(BF16), wglu) + bglu)
    return (z * gate).astype(BF16)


def _ssm_prompt_kernel(u_ref, bb_ref, ccre_ref, ccim_ref, are_ref, aim_ref, dskip_ref,
                       wglu_ref, bglu_ref, y_ref, hre_ref, him_ref,
                       sre, sim, cre, cim, *, tc, nchunk, ch_w, st_w):
    t = pl.program_id(1)

    @pl.when(t == 0)
    def _():
        cre[...] = jnp.zeros_like(cre)
        cim[...] = jnp.zeros_like(cim)

    u = u_ref[...]
    ub = u.astype(BF16)
    for k in range(nchunk):
        bu = _dot(ub[:, k * ch_w:(k + 1) * ch_w], bb_ref[k])
        sre[:, k * st_w:(k + 1) * st_w] = bu[:, :st_w]
        sim[:, k * st_w:(k + 1) * st_w] = bu[:, st_w:]

    for k in range(nchunk):
        cols = slice(k * st_w, (k + 1) * st_w)
        ar, ai = are_ref[:, cols], aim_ref[:, cols]

        def body(r, carry, cols=cols, ar=ar, ai=ai):
            hr, hi = carry
            row = pl.ds(r, 1)
            nr = ar * hr - ai * hi + sre[row, cols]
            ni = ar * hi + ai * hr + sim[row, cols]
            sre[row, cols] = nr
            sim[row, cols] = ni
            return nr, ni

        hr, hi = lax.fori_loop(0, tc, body, (cre[:, cols], cim[:, cols]), unroll=8)
        cre[:, cols] = hr
        cim[:, cols] = hi

    ys = []
    for k in range(nchunk):
        cols = slice(k * st_w, (k + 1) * st_w)
        ys.append(_dot(sre[:, cols].astype(BF16), ccre_ref[k])
                  - _dot(sim[:, cols].astype(BF16), ccim_ref[k]))
    y = jnp.concatenate(ys, axis=1)
    y_ref[...] = _ssm_tail(y, u, dskip_ref[...], wglu_ref[...], bglu_ref[...])
    hre_ref[...] = cre[...]
    him_ref[...] = cim[...]


def ssm_prompt(z, l, batch, seq, bb, ccre, ccim, a_re, a_im, d_skip, w_glu, b_glu):
    rows = z.shape[0]
    nchunk, ch_w, st2 = bb.shape[1:]
    st_w = st2 // 2
    d_ssm = nchunk * ch_w
    n_state = nchunk * st_w
    tc = 256
    nt = seq // tc
    kern = functools.partial(_ssm_prompt_kernel, tc=tc, nchunk=nchunk, ch_w=ch_w, st_w=st_w)
    vec = lambda n: pl.BlockSpec((None, 1, n), lambda b, t: (l, 0, 0))
    return pl.pallas_call(
        kern,
        out_shape=(jax.ShapeDtypeStruct((rows, d_ssm), BF16),
                   jax.ShapeDtypeStruct((batch, 1, n_state), F32),
                   jax.ShapeDtypeStruct((batch, 1, n_state), F32)),
        grid=(batch, nt),
        in_specs=[pl.BlockSpec((tc, d_ssm), lambda b, t: (b * nt + t, 0)),
                  pl.BlockSpec((None, nchunk, ch_w, st2), lambda b, t: (l, 0, 0, 0)),
                  pl.BlockSpec((None, nchunk, st_w, ch_w), lambda b, t: (l, 0, 0, 0)),
                  pl.BlockSpec((None, nchunk, st_w, ch_w), lambda b, t: (l, 0, 0, 0)),
                  vec(n_state), vec(n_state), vec(d_ssm),
                  pl.BlockSpec((None, d_ssm, d_ssm), lambda b, t: (l, 0, 0)),
                  vec(d_ssm)],
        out_specs=(pl.BlockSpec((tc, d_ssm), lambda b, t: (b * nt + t, 0)),
                   pl.BlockSpec((None, 1, n_state), lambda b, t: (b, 0, 0)),
                   pl.BlockSpec((None, 1, n_state), lambda b, t: (b, 0, 0))),
        scratch_shapes=[pltpu.VMEM((tc, n_state), F32), pltpu.VMEM((tc, n_state), F32),
                        pltpu.VMEM((1, n_state), F32), pltpu.VMEM((1, n_state), F32)],
        compiler_params=_cparams("arbitrary", "arbitrary"),
        name="ssm_prompt",
    )(z, bb, ccre, ccim, a_re, a_im, d_skip, w_glu, b_glu)


def _ssm_sample_kernel(u_ref, h0re_ref, h0im_ref, bbhi_ref, bblo_ref, ccre_ref, ccim_ref,
                       are_ref, aim_ref, dskip_ref, wglu_ref, bglu_ref,
                       y_ref, hre_ref, him_ref, *, nchunk, ch_w, st_w):
    u = u_ref[...]
    u_hi = u.astype(BF16)
    u_lo = (u - u_hi.astype(F32)).astype(BF16)
    ys = []
    for k in range(nchunk):
        cols = slice(k * st_w, (k + 1) * st_w)
        uh, ul = u_hi[:, k * ch_w:(k + 1) * ch_w], u_lo[:, k * ch_w:(k + 1) * ch_w]
        bu = _dot(uh, bbhi_ref[k]) + (_dot(uh, bblo_ref[k]) + _dot(ul, bbhi_ref[k]))
        ar, ai = are_ref[:, cols], aim_ref[:, cols]
        h0r, h0i = h0re_ref[:, cols], h0im_ref[:, cols]
        nr = ar * h0r - ai * h0i + bu[:, :st_w]
        ni = ar * h0i + ai * h0r + bu[:, st_w:]
        hre_ref[:, cols] = nr
        him_ref[:, cols] = ni
        ys.append(_dot(nr.astype(BF16), ccre_ref[k]) - _dot(ni.astype(BF16), ccim_ref[k]))
    y = jnp.concatenate(ys, axis=1)
    y_ref[...] = _ssm_tail(y, u, dskip_ref[...], wglu_ref[...], bglu_ref[...])


def ssm_sample(z, l, h0_re, h0_im, bb_hi, bb_lo, ccre, ccim, a_re, a_im, d_skip, w_glu, b_glu):
    rows = z.shape[0]
    nchunk, ch_w, st2 = bb_hi.shape[1:]
    st_w = st2 // 2
    d_ssm = nchunk * ch_w
    n_state = nchunk * st_w
    kern = functools.partial(_ssm_sample_kernel, nchunk=nchunk, ch_w=ch_w, st_w=st_w)
    vec = lambda n: pl.BlockSpec((None, 1, n), lambda i: (l, 0, 0))
    full = lambda a, b, c: pl.BlockSpec((None, a, b, c), lambda i: (l, 0, 0, 0))
    state = pl.BlockSpec((None, rows, n_state), lambda i: (l, 0, 0))
    return pl.pallas_call(
        kern,
        out_shape=(jax.ShapeDtypeStruct((rows, d_ssm), BF16),
                   jax.ShapeDtypeStruct((rows, n_state), F32),
                   jax.ShapeDtypeStruct((rows, n_state), F32)),
        grid=(1,),
        in_specs=[pl.BlockSpec((rows, d_ssm), lambda i: (0, 0)), state, state,
                  full(nchunk, ch_w, st2), full(nchunk, ch_w, st2),
                  full(nchunk, st_w, ch_w), full(nchunk, st_w, ch_w),
                  vec(n_state), vec(n_state), vec(d_ssm),
                  pl.BlockSpec((None, d_ssm, d_ssm), lambda i: (l, 0, 0)), vec(d_ssm)],
        out_specs=(pl.BlockSpec((rows, d_ssm), lambda i: (0, 0)),
                   pl.BlockSpec((rows, n_state), lambda i: (0, 0)),
                   pl.BlockSpec((rows, n_state), lambda i: (0, 0))),
        compiler_params=_cparams("arbitrary"),
        name="ssm_sample",
    )(z, h0_re, h0_im, bb_hi, bb_lo, ccre, ccim, a_re, a_im, d_skip, w_glu, b_glu)


def _q_kernel(ql_ref, g_ref, wa_ref, wb_ref, cos_ref, sin_ref, gq_ref, o_ref, *, heads, qk_dim):
    qn = (_rms(ql_ref[...]) * g_ref[...]).astype(BF16)
    qa = _dot(qn, wa_ref[...])
    qb = _dot(qn, wb_ref[...])
    cos, sin = cos_ref[...], sin_ref[...]
    g1, g2 = gq_ref[:, :NOPE], gq_ref[:, NOPE:]
    for h in range(heads):
        n = qa[:, h * HEAD_PAD:h * HEAD_PAD + NOPE]
        r = (qa[:, h * HEAD_PAD + NOPE:(h + 1) * HEAD_PAD] * cos
             + qb[:, h * NOPE:(h + 1) * NOPE] * sin)
        ssq = jnp.sum(n * n, axis=-1, keepdims=True) + jnp.sum(r * r, axis=-1, keepdims=True)
        inv = lax.rsqrt(ssq * (1.0 / qk_dim) + EPS)
        o_ref[:, h * HEAD_PAD:h * HEAD_PAD + NOPE] = (n * inv * g1).astype(BF16)
        o_ref[:, h * HEAD_PAD + NOPE:(h + 1) * HEAD_PAD] = (r * inv * g2).astype(BF16)


def mla_q(z, l, tm, q_col, g_q_lat, wqa, wqb, cos_t, sin_t, gq, heads, qk_dim):
    rows = z.shape[0]
    q_lora = wqa.shape[1]
    ntab = cos_t.shape[0] // tm
    kern = functools.partial(_q_kernel, heads=heads, qk_dim=qk_dim)
    return pl.pallas_call(
        kern,
        out_shape=jax.ShapeDtypeStruct((rows, heads * HEAD_PAD), BF16),
        grid=(rows // tm,),
        in_specs=[pl.BlockSpec((tm, q_lora), lambda i: (i, q_col)),
                  pl.BlockSpec((None, 1, q_lora), lambda i: (l, 0, 0)),
                  pl.BlockSpec((None, q_lora, heads * HEAD_PAD), lambda i: (l, 0, 0)),
                  pl.BlockSpec((None, q_lora, heads * NOPE), lambda i: (l, 0, 0)),
                  pl.BlockSpec((tm, NOPE), lambda i: (i % ntab, 0)),
                  pl.BlockSpec((tm, NOPE), lambda i: (i % ntab, 0)),
                  pl.BlockSpec((None, 1, HEAD_PAD), lambda i: (l, 0, 0))],
        out_specs=pl.BlockSpec((tm, heads * HEAD_PAD), lambda i: (i, 0)),
        compiler_params=_cparams("arbitrary"),
        name="mla_q",
    )(z, g_q_lat, wqa, wqb, cos_t, sin_t, gq)


def _kv_kernel(c_ref, kr_ref, g_ref, wk_ref, wv_ref, cos_ref, sin_ref, gk_ref,
               rows_ref, k_ref, v_ref, *, heads, qk_dim, kv_lora):
    cn = _rms(c_ref[...]) * g_ref[...]
    zkr = kr_ref[...]
    krr = zkr * cos_ref[...] + pltpu.roll(zkr, ROPE, axis=1) * sin_ref[...]
    rows_ref[:, :kv_lora] = cn
    rows_ref[:, kv_lora:] = krr[:, :ROPE]
    cb = cn.astype(BF16)
    kn = _dot(cb, wk_ref[...])
    v_ref[...] = _dot(cb, wv_ref[...]).astype(BF16)
    kr_sq = jnp.sum(krr * krr, axis=-1, keepdims=True)
    g1, g2 = gk_ref[:, :NOPE], gk_ref[:, NOPE:]
    krg = krr * g2
    for h in range(heads):
        n = kn[:, h * NOPE:(h + 1) * NOPE]
        inv = lax.rsqrt((jnp.sum(n * n, axis=-1, keepdims=True) + kr_sq) * (1.0 / qk_dim) + EPS)
        k_ref[:, h * HEAD_PAD:h * HEAD_PAD + NOPE] = (n * inv * g1).astype(BF16)
        k_ref[:, h * HEAD_PAD + NOPE:(h + 1) * HEAD_PAD] = (krg * inv).astype(BF16)


def mla_kv(z, zkr, l, tm, c_col, g_kv_lat, wk, wv, cos_t, sin_t, gk, heads, qk_dim):
    rows = z.shape[0]
    kv_lora = wk.shape[1]
    ntab = cos_t.shape[0] // tm
    kern = functools.partial(_kv_kernel, heads=heads, qk_dim=qk_dim, kv_lora=kv_lora)
    return pl.pallas_call(
        kern,
        out_shape=(jax.ShapeDtypeStruct((rows, kv_lora + ROPE), F32),
                   jax.ShapeDtypeStruct((rows, heads * HEAD_PAD), BF16),
                   jax.ShapeDtypeStruct((rows, heads * NOPE), BF16)),
        grid=(rows // tm,),
        in_specs=[pl.BlockSpec((tm, kv_lora), lambda i: (i, c_col)),
                  pl.BlockSpec((tm, 2 * ROPE), lambda i: (i, 0)),
                  pl.BlockSpec((None, 1, kv_lora), lambda i: (l, 0, 0)),
                  pl.BlockSpec((None, kv_lora, heads * NOPE), lambda i: (l, 0, 0)),
                  pl.BlockSpec((None, kv_lora, heads * NOPE), lambda i: (l, 0, 0)),
                  pl.BlockSpec((tm, NOPE), lambda i: (i % ntab, 0)),
                  pl.BlockSpec((tm, NOPE), lambda i: (i % ntab, 0)),
                  pl.BlockSpec((None, 1, HEAD_PAD), lambda i: (l, 0, 0))],
        out_specs=(pl.BlockSpec((tm, kv_lora + ROPE), lambda i: (i, 0)),
                   pl.BlockSpec((tm, heads * HEAD_PAD), lambda i: (i, 0)),
                   pl.BlockSpec((tm, heads * NOPE), lambda i: (i, 0))),
        compiler_params=_cparams("arbitrary"),
        name="mla_kv",
    )(z, zkr, g_kv_lat, wk, wv, cos_t, sin_t, gk)


def _flash_kernel(q_ref, k_ref, v_ref, o_ref, m_sc, l_sc, acc_sc, *, tq, tk, nk):
    i, j = pl.program_id(2), pl.program_id(3)

    @pl.when(j == 0)
    def _():
        m_sc[...] = jnp.full_like(m_sc, -jnp.inf)
        l_sc[...] = jnp.zeros_like(l_sc)
        acc_sc[...] = jnp.zeros_like(acc_sc)

    @pl.when(j <= i)
    def _():
        s = _dot_nt(q_ref[...], k_ref[...])
        qpos = i * tq + lax.broadcasted_iota(jnp.int32, (tq, tk), 0)
        kpos = j * tk + lax.broadcasted_iota(jnp.int32, (tq, tk), 1)
        s = jnp.where(kpos <= qpos, s, -jnp.inf)
        m_prev = m_sc[...]
        m_new = jnp.maximum(m_prev, jnp.max(s, axis=-1, keepdims=True))
        alpha = jnp.exp(m_prev - m_new)
        p = jnp.exp(s - m_new[:, :1])
        l_sc[...] = alpha * l_sc[...] + jnp.sum(p, axis=-1, keepdims=True)
        acc_sc[...] = alpha[:, :1] * acc_sc[...] + _dot(p.astype(BF16), v_ref[...])
        m_sc[...] = m_new

    @pl.when(j == nk - 1)
    def _():
        o_ref[...] = (acc_sc[...] / l_sc[:, :1]).astype(BF16)


def flash_prompt(q, k, v, batch, seq, heads):
    rows = q.shape[0]
    tq = tk = 512
    nq, nk = seq // tq, seq // tk
    kern = functools.partial(_flash_kernel, tq=tq, tk=tk, nk=nk)
    return pl.pallas_call(
        kern,
        out_shape=jax.ShapeDtypeStruct((rows, heads * NOPE), BF16),
        grid=(batch, heads, nq, nk),
        in_specs=[pl.BlockSpec((tq, HEAD_PAD), lambda b, h, i, j: (b * nq + i, h)),
                  pl.BlockSpec((tk, HEAD_PAD), lambda b, h, i, j: (b * nk + jnp.minimum(j, i), h)),
                  pl.BlockSpec((tk, NOPE), lambda b, h, i, j: (b * nk + jnp.minimum(j, i), h))],
        out_specs=pl.BlockSpec((tq, NOPE), lambda b, h, i, j: (b * nq + i, h)),
        scratch_shapes=[pltpu.VMEM((tq, NOPE), F32), pltpu.VMEM((tq, NOPE), F32),
                        pltpu.VMEM((tq, NOPE), F32)],
        compiler_params=_cparams("arbitrary", "arbitrary", "arbitrary", "arbitrary"),
        name="flash_prompt",
    )(q, k, v)


def _absorb_kernel(q_ref, k_ref, wk_ref, gk_ref, qabs_ref, qr_ref, sn_ref):
    q = q_ref[...].astype(F32)
    k = k_ref[...].astype(F32)
    sn = jnp.sum(q * k, axis=-1, keepdims=True)
    sn_ref[...] = jnp.broadcast_to(sn, sn_ref.shape)
    qg = q * gk_ref[...]
    qabs_ref[...] = _dot_nt(qg[:, :NOPE].astype(BF16), wk_ref[...]).astype(BF16)
    qr_ref[...] = qg[:, NOPE:].astype(BF16)


def mla_absorb(q, k_new, l, wk, gk, heads):
    rows = q.shape[0]
    kv_lora = wk.shape[1]
    return pl.pallas_call(
        _absorb_kernel,
        out_shape=(jax.ShapeDtypeStruct((heads, rows, kv_lora), BF16),
                   jax.ShapeDtypeStruct((heads, rows, NOPE), BF16),
                   jax.ShapeDtypeStruct((heads, rows, NOPE), F32)),
        grid=(heads,),
        in_specs=[pl.BlockSpec((rows, HEAD_PAD), lambda h: (0, h)),
                  pl.BlockSpec((rows, HEAD_PAD), lambda h: (0, h)),
                  pl.BlockSpec((None, kv_lora, NOPE), lambda h: (l, 0, h)),
                  pl.BlockSpec((None, 1, HEAD_PAD), lambda h: (l, 0, 0))],
        out_specs=(pl.BlockSpec((None, rows, kv_lora), lambda h: (h, 0, 0)),
                   pl.BlockSpec((None, rows, NOPE), lambda h: (h, 0, 0)),
                   pl.BlockSpec((None, rows, NOPE), lambda h: (h, 0, 0))),
        compiler_params=_cparams("arbitrary"),
        name="mla_absorb",
    )(q, k_new, wk, gk)


DECODE_PAGES_PER_STEP = 8


def _decode_kernel(pt_ref, *refs, heads, kv_lora, qk_dim, nsteps, npg):
    del pt_ref
    page_refs = refs[:npg]
    wkt_ref, qabs_ref, qr_ref, sn_ref, cnew_ref, o_ref, m_sc, l_sc, acc_sc = refs[npg:]
    j = pl.program_id(1)

    @pl.when(j == 0)
    def _():
        m_sc[...] = sn_ref[...]
        l_sc[...] = jnp.ones_like(l_sc)
        acc_sc[...] = jnp.broadcast_to(cnew_ref[...], acc_sc.shape)

    ct = jnp.concatenate([r[0:kv_lora, :].astype(BF16) for r in page_refs], axis=1)
    krt = jnp.concatenate([r[kv_lora:, :] for r in page_refs], axis=1)
    nrow = ct.shape[1]
    kt = _dot(wkt_ref[...], ct)
    ssq = jnp.sum((kt * kt).reshape(heads, NOPE, nrow), axis=1)
    kr_sq = jnp.sum(krt * krt, axis=0, keepdims=True)
    inv = lax.rsqrt((ssq + kr_sq) * (1.0 / qk_dim) + EPS)
    s = (_dot(qabs_ref[...], ct) + _dot(qr_ref[...], krt.astype(BF16))) * inv

    m_prev = m_sc[...]
    m_new = jnp.maximum(m_prev, jnp.max(s, axis=-1, keepdims=True))
    alpha = jnp.exp(m_prev - m_new)
    p = jnp.exp(s - m_new[:, :1])
    l_sc[...] = alpha * l_sc[...] + jnp.sum(p, axis=-1, keepdims=True)
    acc_sc[...] = alpha[:, :1] * acc_sc[...] + _dot_nt(p.astype(BF16), ct)
    m_sc[...] = m_new

    @pl.when(j == nsteps - 1)
    def _():
        o_ref[...] = acc_sc[...] / l_sc[:, :1]


def mla_decode(cache_t, page_table, l, wkt, qabs, qr, sn, c_new, heads, qk_dim):
    bd, npages = page_table.shape
    kv_row = cache_t.shape[2]
    kv_lora = kv_row - ROPE
    npg = DECODE_PAGES_PER_STEP
    nsteps = npages // npg
    kern = functools.partial(_decode_kernel, heads=heads, kv_lora=kv_lora, qk_dim=qk_dim,
                             nsteps=nsteps, npg=npg)
    page_specs = [pl.BlockSpec((None, None, kv_row, PAGE),
                               lambda b, j, pt, i=i: (l, pt[b, npg * j + i], 0, 0))
                  for i in range(npg)]
    grid_spec = pltpu.PrefetchScalarGridSpec(
        num_scalar_prefetch=1,
        grid=(bd, nsteps),
        in_specs=page_specs + [
                  pl.BlockSpec((None, heads * NOPE, kv_lora), lambda b, j, pt: (l, 0, 0)),
                  pl.BlockSpec((None, heads, kv_lora), lambda b, j, pt: (b, 0, 0)),
                  pl.BlockSpec((None, heads, ROPE), lambda b, j, pt: (b, 0, 0)),
                  pl.BlockSpec((None, heads, NOPE), lambda b, j, pt: (b, 0, 0)),
                  pl.BlockSpec((None, 1, kv_lora), lambda b, j, pt: (b, 0, 0))],
        out_specs=pl.BlockSpec((None, heads, kv_lora), lambda b, j, pt: (b, 0, 0)),
        scratch_shapes=[pltpu.VMEM((heads, NOPE), F32), pltpu.VMEM((heads, NOPE), F32),
                        pltpu.VMEM((heads, kv_lora), F32)],
    )
    return pl.pallas_call(
        kern,
        out_shape=jax.ShapeDtypeStruct((bd, heads, kv_lora), F32),
        grid_spec=grid_spec,
        compiler_params=_cparams("arbitrary", "arbitrary"),
        name="mla_decode",
    )(page_table, *([cache_t] * npg), wkt, qabs, qr, sn, c_new)


def _vup_kernel(o_ref, wv_ref, out_ref):
    out_ref[...] = _dot(o_ref[...].astype(BF16), wv_ref[...]).astype(BF16)


def mla_vup(o_lat_t, l, wv):
    heads, rows, kv_lora = o_lat_t.shape
    return pl.pallas_call(
        _vup_kernel,
        out_shape=jax.ShapeDtypeStruct((rows, heads * NOPE), BF16),
        grid=(heads,),
        in_specs=[pl.BlockSpec((None, rows, kv_lora), lambda h: (h, 0, 0)),
                  pl.BlockSpec((None, kv_lora, NOPE), lambda h: (l, 0, h))],
        out_specs=pl.BlockSpec((rows, NOPE), lambda h: (0, h)),
        compiler_params=_cparams("arbitrary"),
        name="mla_vup",
    )(o_lat_t, wv)


def _merge_kernel(y_ref, o_ref, ga_ref, gb_ref, ws_ref, wa_ref, out_ref):
    a = _dot(y_ref[...], ws_ref[...])
    b = _dot(o_ref[...], wa_ref[...])
    out_ref[...] = (jax.nn.sigmoid(ga_ref[...]) * a + jax.nn.sigmoid(gb_ref[...]) * b).astype(BF16)


def merge(y, o, z, l, tm, tn, ga_col0, gb_col0, w_ssm_out, w_att_out):
    rows, d_ssm = y.shape
    d_att = o.shape[1]
    d = w_ssm_out.shape[-1]
    ga0, gb0 = ga_col0 // tn, gb_col0 // tn
    return pl.pallas_call(
        _merge_kernel,
        out_shape=jax.ShapeDtypeStruct((rows, d), BF16),
        grid=(d // tn, rows // tm),
        in_specs=[pl.BlockSpec((tm, d_ssm), lambda j, i: (i, 0)),
                  pl.BlockSpec((tm, d_att), lambda j, i: (i, 0)),
                  pl.BlockSpec((tm, tn), lambda j, i: (i, ga0 + j)),
                  pl.BlockSpec((tm, tn), lambda j, i: (i, gb0 + j)),
                  pl.BlockSpec((None, d_ssm, tn), lambda j, i: (l, 0, j)),
                  pl.BlockSpec((None, d_att, tn), lambda j, i: (l, 0, j))],
        out_specs=pl.BlockSpec((tm, tn), lambda j, i: (i, j)),
        compiler_params=_cparams("arbitrary", "arbitrary"),
        name="merge",
    )(y, o, z, z, w_ssm_out, w_att_out)


def _oproj_kernel(m_ref, w_ref, x_ref, gt_ref, g_ref, sh_ref, sc_ref, x1_ref, h_ref):
    x1 = x_ref[...] + gt_ref[...] * _dot(m_ref[...], w_ref[...])
    x1_ref[...] = x1
    h = _rms(x1) * g_ref[...]
    h_ref[...] = (h * (1.0 + sc_ref[...]) + sh_ref[...]).astype(BF16)


def oproj(st, merged, x, l, w_o, g_ffn):
    rows, d = x.shape
    tm = min(st.tm, 512)
    st2 = Stream(rows, tm, st.mod, st.seq)
    ident = lambda i: i
    return pl.pallas_call(
        _oproj_kernel,
        out_shape=(jax.ShapeDtypeStruct((rows, d), F32), jax.ShapeDtypeStruct((rows, d), BF16)),
        grid=(rows // tm,),
        in_specs=[pl.BlockSpec((tm, d), lambda i: (i, 0)),
                  pl.BlockSpec((None, d, d), lambda i: (l, 0, 0)),
                  pl.BlockSpec((tm, d), lambda i: (i, 0)),
                  st2.mod_spec(l, 2, d, ident),
                  pl.BlockSpec((None, 1, d), lambda i: (l, 0, 0)),
                  st2.mod_spec(l, 3, d, ident),
                  st2.mod_spec(l, 4, d, ident)],
        out_specs=(pl.BlockSpec((tm, d), lambda i: (i, 0)), pl.BlockSpec((tm, d), lambda i: (i, 0))),
        compiler_params=_cparams("arbitrary"),
        name="oproj",
    )(merged, w_o, x, st.mod, g_ffn, st.mod, st.mod)


def _conv3(s_ref, cw_ref, cb_ref, tm):
    return (cb_ref[...] + cw_ref[0:1, :] * s_ref[pl.ds(6, tm), :]
            + cw_ref[1:2, :] * s_ref[pl.ds(7, tm), :] + cw_ref[2:3, :] * s_ref[pl.ds(8, tm), :])


def _ffn_up_prompt_kernel(h_ref, wg_ref, wv_ref, cwg_ref, cwv_ref, cbg_ref, cbv_ref,
                          act_ref, tg_ref, tv_ref, sg, sv, *, tm, tiles_per_batch):
    i = pl.program_id(1)

    @pl.when(i % tiles_per_batch == 0)
    def _():
        sg[0:8, :] = jnp.zeros((8, sg.shape[1]), F32)
        sv[0:8, :] = jnp.zeros((8, sv.shape[1]), F32)

    h = h_ref[...]
    sg[pl.ds(8, tm), :] = _dot(h, wg_ref[...])
    sv[pl.ds(8, tm), :] = _dot(h, wv_ref[...])
    zg = _conv3(sg, cwg_ref, cbg_ref, tm)
    zv = _conv3(sv, cwv_ref, cbv_ref, tm)
    act_ref[...] = (zg * jax.nn.sigmoid(zg) * zv).astype(BF16)
    tail_g = sg[pl.ds(tm, 8), :]
    tail_v = sv[pl.ds(tm, 8), :]
    tg_ref[...] = tail_g
    tv_ref[...] = tail_v
    sg[0:8, :] = tail_g
    sv[0:8, :] = tail_v


def ffn_up_prompt(h, l, batch, seq, w_up, conv_w, conv_b):
    rows, d = h.shape
    d_ff = w_up.shape[-1] // 2
    tm, tn = 1024, 512
    nj = d_ff // tn
    tpb = seq // tm
    kern = functools.partial(_ffn_up_prompt_kernel, tm=tm, tiles_per_batch=tpb)
    return pl.pallas_call(
        kern,
        out_shape=(jax.ShapeDtypeStruct((rows, d_ff), BF16),
                   jax.ShapeDtypeStruct((batch, 8, d_ff), F32),
                   jax.ShapeDtypeStruct((batch, 8, d_ff), F32)),
        grid=(nj, rows // tm),
        in_specs=[pl.BlockSpec((tm, d), lambda j, i: (i, 0)),
                  pl.BlockSpec((None, d, tn), lambda j, i: (l, 0, j)),
                  pl.BlockSpec((None, d, tn), lambda j, i: (l, 0, nj + j)),
                  pl.BlockSpec((None, 3, tn), lambda j, i: (l, 0, j)),
                  pl.BlockSpec((None, 3, tn), lambda j, i: (l, 0, nj + j)),
                  pl.BlockSpec((None, 1, tn), lambda j, i: (l, 0, j)),
                  pl.BlockSpec((None, 1, tn), lambda j, i: (l, 0, nj + j))],
        out_specs=(pl.BlockSpec((tm, tn), lambda j, i: (i, j)),
                   pl.BlockSpec((None, 8, tn), lambda j, i: (i // tpb, 0, j)),
                   pl.BlockSpec((None, 8, tn), lambda j, i: (i // tpb, 0, j))),
        scratch_shapes=[pltpu.VMEM((tm + 8, tn), F32), pltpu.VMEM((tm + 8, tn), F32)],
        compiler_params=_cparams("arbitrary", "arbitrary"),
        name="ffn_up_prompt",
    )(h, w_up, w_up, conv_w, conv_w, conv_b, conv_b)


def _ffn_up_sample_kernel(h_ref, wg_ref, wv_ref, cwg_ref, cwv_ref, cbg_ref, cbv_ref,
                          g0_ref, g1_ref, v0_ref, v1_ref, act_ref, ug_ref, uv_ref):
    h = h_ref[...]
    ug = _dot(h, wg_ref[...])
    uv = _dot(h, wv_ref[...])
    ug_ref[...] = ug
    uv_ref[...] = uv
    zg = (cbg_ref[...] + g0_ref[...] * cwg_ref[0:1, :] + g1_ref[...] * cwg_ref[1:2, :]
          + ug * cwg_ref[2:3, :])
    zv = (cbv_ref[...] + v0_ref[...] * cwv_ref[0:1, :] + v1_ref[...] * cwv_ref[1:2, :]
          + uv * cwv_ref[2:3, :])
    act_ref[...] = (zg * jax.nn.sigmoid(zg) * zv).astype(BF16)


def ffn_up_sample(h, l, w_up, conv_w, conv_b, buf0, buf1):
    rows, d = h.shape
    d_ff = w_up.shape[-1] // 2
    tn = 512
    nj = d_ff // tn
    col = lambda off: pl.BlockSpec((rows, tn), lambda j: (0, off + j))
    return pl.pallas_call(
        _ffn_up_sample_kernel,
        out_shape=(jax.ShapeDtypeStruct((rows, d_ff), BF16),
                   jax.ShapeDtypeStruct((rows, d_ff), F32),
                   jax.ShapeDtypeStruct((rows, d_ff), F32)),
        grid=(nj,),
        in_specs=[pl.BlockSpec((rows, d), lambda j: (0, 0)),
                  pl.BlockSpec((None, d, tn), lambda j: (l, 0, j)),
                  pl.BlockSpec((None, d, tn), lambda j: (l, 0, nj + j)),
                  pl.BlockSpec((None, 3, tn), lambda j: (l, 0, j)),
                  pl.BlockSpec((None, 3, tn), lambda j: (l, 0, nj + j)),
                  pl.BlockSpec((None, 1, tn), lambda j: (l, 0, j)),
                  pl.BlockSpec((None, 1, tn), lambda j: (l, 0, nj + j)),
                  col(0), col(0), col(nj), col(nj)],
        out_specs=(col(0), col(0), col(0)),
        compiler_params=_cparams("arbitrary"),
        name="ffn_up_sample",
    )(h, w_up, w_up, conv_w, conv_w, conv_b, conv_b, buf0, buf1, buf0, buf1)


def _ffn_down_kernel(a_ref, w_ref, x_ref, gt_ref, o_ref):
    o_ref[...] = x_ref[...] + gt_ref[...] * _dot(a_ref[...], w_ref[...])


def ffn_down(st, act, x1, l, w_down):
    rows, d_ff = act.shape
    d = x1.shape[1]
    tm = min(st.tm, 512)
    tn = 512
    st2 = Stream(rows, tm, st.mod, st.seq)
    ncol = d // tn
    if st2.r == 1:
        gt_spec = pl.BlockSpec((None, None, 1, tn),
                               lambda j, i: (l, (i * tm) // st2.seq, 0, 5 * ncol + j))
    else:
        gt_spec = pl.BlockSpec((None, None, st2.r, tn), lambda j, i: (l, 0, 0, 5 * ncol + j))
    return pl.pallas_call(
        _ffn_down_kernel,
        out_shape=jax.ShapeDtypeStruct((rows, d), F32),
        grid=(ncol, rows // tm),
        in_specs=[pl.BlockSpec((tm, d_ff), lambda j, i: (i, 0)),
                  pl.BlockSpec((None, d_ff, tn), lambda j, i: (l, 0, j)),
                  pl.BlockSpec((tm, tn), lambda j, i: (i, j)),
                  gt_spec],
        out_specs=pl.BlockSpec((tm, tn), lambda j, i: (i, j)),
        compiler_params=_cparams("arbitrary", "arbitrary"),
        name="ffn_down",
    )(act, w_down, x1, st.mod)


def _rot_cols(w):
    half = w.shape[-1] // 2
    return jnp.concatenate([-w[..., half:], w[..., :half]], axis=-1)


def _rope_tables(pos, rows):
    half = ROPE // 2
    inv = ROPE_THETA ** (-jnp.arange(half, dtype=F32) / half)
    ang = pos.astype(F32)[:, None] * inv[None, :]
    cos, sin = jnp.cos(ang), jnp.sin(ang)
    pad = jnp.zeros((pos.shape[0], NOPE - ROPE), F32)
    cos_t = jnp.concatenate([cos, cos, pad], axis=-1)
    sin_t = jnp.concatenate([sin, sin, pad], axis=-1)
    if cos_t.shape[0] != rows:
        cos_t = jnp.broadcast_to(cos_t, (rows, NOPE))
        sin_t = jnp.broadcast_to(sin_t, (rows, NOPE))
    return cos_t, sin_t


def kernel(x_prompt, x_sample, c_prompt, c_sample, cache_kv, state_ssm_re, state_ssm_im, state_conv, page_table, w_ada, b_ada, g_norm_mix, g_norm_ffn, w_in, lam_re, lam_im, log_step, b_re, b_im, c_re, c_im, d_skip, w_glu, b_glu, w_ssm_out, g_q_lat, w_uq, g_kv_lat, w_uk, w_uv, g_qk_q, g_qk_k, w_att_out, w_o, w_up, conv_w, conv_b, w_down):
    batch, seq, d = x_prompt.shape
    bd = x_sample.shape[0]
    depth = w_in.shape[0]
    d_ssm = d_skip.shape[-1]
    q_lora = w_uq.shape[1]
    kv_lora, heads, nope = w_uk.shape[1:]
    qk_dim = g_qk_q.shape[-1]
    rope = qk_dim - nope
    n_groups, state_p = lam_re.shape[1:]
    d_ff = w_down.shape[1]
    assert (nope, rope) == (NOPE, ROPE) and cache_kv.shape[2] == PAGE and x_sample.shape[1] == 1
    past_len = page_table.shape[1] * PAGE
    scale = qk_dim ** -0.5

    c_all = jnp.concatenate([c_sample, c_prompt, jnp.zeros((8 - batch % 8, d), F32)], axis=0)
    mod = ada_modulation(c_all, w_ada, b_ada)
    st_p = Stream(batch * seq, 1024, mod[:, bd:bd + batch].reshape(depth, batch, 1, 6 * d), seq)
    st_s = Stream(bd, bd, mod[:, :bd].reshape(depth, 1, bd, 6 * d), 1)

    vec3 = lambda a: a.reshape(depth, 1, a.shape[-1])
    s0, s1, s2, s3, s4 = d_ssm, d_ssm + q_lora, d_ssm + q_lora + kv_lora, \
        d_ssm + q_lora + kv_lora + rope, d_ssm + q_lora + kv_lora + rope + d
    w_main = jnp.concatenate([w_in[..., :s2], w_in[..., s3:]], axis=-1).astype(BF16)
    w_kr = w_in[..., s2:s3]
    w_kr = jnp.concatenate([w_kr, _rot_cols(w_kr)], axis=-1).astype(BF16)
    q_col, c_col = s0 // q_lora, s1 // kv_lora
    ga_col0, gb_col0 = s2, s2 + d

    wq = w_uq.reshape(depth, q_lora, heads, qk_dim)
    wq_n, wq_r = wq[..., :nope], wq[..., nope:]
    zpad = jnp.zeros((depth, q_lora, heads, HEAD_PAD - qk_dim), F32)
    wqa = jnp.concatenate([wq_n, wq_r, zpad], axis=-1).reshape(depth, q_lora, heads * HEAD_PAD).astype(BF16)
    wqb = jnp.concatenate([_rot_cols(wq_r), zpad], axis=-1).reshape(depth, q_lora, heads * NOPE).astype(BF16)
    wk2 = w_uk.reshape(depth, kv_lora, heads * nope).astype(BF16)
    wv2 = w_uv.reshape(depth, kv_lora, heads * nope).astype(BF16)
    wkt = wk2.transpose(0, 2, 1)
    gpad = jnp.zeros((depth, HEAD_PAD - qk_dim), F32)
    gq = vec3(jnp.concatenate([g_qk_q, gpad], axis=-1) * scale)
    gk = vec3(jnp.concatenate([g_qk_k, gpad], axis=-1))

    a_re, a_im, bbt_re, bbt_im = ssm_params(lam_re, lam_im, log_step, b_re, b_im)
    bb = jnp.concatenate([_block_diag_in(bbt_re), _block_diag_in(bbt_im)], axis=-1)
    bb_hi = bb.astype(BF16)
    bb_lo = (bb - bb_hi.astype(F32)).astype(BF16)
    ccre = _block_diag_out(c_re).astype(BF16)
    ccim = _block_diag_out(c_im).astype(BF16)
    n_state = n_groups * state_p
    a_re3, a_im3 = a_re.reshape(depth, 1, n_state), a_im.reshape(depth, 1, n_state)

    w_glu_b, w_sso_b, w_ao_b = w_glu.astype(BF16), w_ssm_out.astype(BF16), w_att_out.astype(BF16)
    w_o_b, w_up_b, w_down_b = w_o.astype(BF16), w_up.astype(BF16), w_down.astype(BF16)
    g_mix3, g_ffn3, g_ql3, g_kv3 = vec3(g_norm_mix), vec3(g_norm_ffn), vec3(g_q_lat), vec3(g_kv_lat)
    d_skip3, b_glu3, conv_b3 = vec3(d_skip), vec3(b_glu), vec3(conv_b)

    cos_p, sin_p = _rope_tables(jnp.arange(seq), seq)
    cos_s, sin_s = _rope_tables(jnp.full((1,), past_len), bd)

    xp = x_prompt.reshape(batch * seq, d)
    xs = x_sample.reshape(bd, d)
    h0_re = state_ssm_re.reshape(depth, bd, n_state)
    h0_im = state_ssm_im.reshape(depth, bd, n_state)
    cache_t = jnp.swapaxes(cache_kv, 2, 3)

    outs = {k: [] for k in ("kv_p", "kv_s", "re_p", "im_p", "re_s", "im_s", "cv_p", "cv_s")}
    for l in range(depth):
        h = norm_mod(st_p, xp, g_mix3, l, 0, 1)
        z = matmul(h, w_main, l, 1024, 1024)
        zkr = matmul(h, w_kr, l, 1024, 2 * ROPE)
        y_ssm, hre, him = ssm_prompt(z, l, batch, seq, bb_hi, ccre, ccim, a_re3, a_im3,
                                     d_skip3, w_glu_b, b_glu3)
        q = mla_q(z, l, 512, q_col, g_ql3, wqa, wqb, cos_p, sin_p, gq, heads, qk_dim)
        rows_p, k, v = mla_kv(z, zkr, l, 512, c_col, g_kv3, wk2, wv2, cos_p, sin_p, gk, heads, qk_dim)
        o = flash_prompt(q, k, v, batch, seq, heads)
        merged = merge(y_ssm, o, z, l, 1024, 512, ga_col0, gb_col0, w_sso_b, w_ao_b)
        x1, h2 = oproj(st_p, merged, xp, l, w_o_b, g_ffn3)
        act, tail_g, tail_v = ffn_up_prompt(h2, l, batch, seq, w_up_b, conv_w, conv_b3)
        xp = ffn_down(st_p, act, x1, l, w_down_b)
        outs["kv_p"].append(rows_p.reshape(batch, seq, kv_lora + rope))
        outs["re_p"].append(hre.reshape(batch, n_groups, state_p))
        outs["im_p"].append(him.reshape(batch, n_groups, state_p))
        outs["cv_p"].append(jnp.concatenate([tail_g[:, 6:], tail_v[:, 6:]], axis=-1))

        h = norm_mod(st_s, xs, g_mix3, l, 0, 1)
        z = matmul(h, w_main, l, bd, 1024)
        zkr = matmul(h, w_kr, l, bd, 2 * ROPE)
        y_ssm, hre, him = ssm_sample(z, l, h0_re, h0_im, bb_hi, bb_lo, ccre, ccim, a_re3, a_im3,
                                     d_skip3, w_glu_b, b_glu3)
        q = mla_q(z, l, bd, q_col, g_ql3, wqa, wqb, cos_s, sin_s, gq, heads, qk_dim)
        rows_s, k_new, _ = mla_kv(z, zkr, l, bd, c_col, g_kv3, wk2, wv2, cos_s, sin_s, gk, heads, qk_dim)
        qabs, qr, sn = mla_absorb(q, k_new, l, wk2, gk, heads)
        o_lat = mla_decode(cache_t, page_table, l, wkt,
                           qabs.transpose(1, 0, 2), qr[:, :, :ROPE].transpose(1, 0, 2),
                           sn.transpose(1, 0, 2), rows_s[:, None, :kv_lora], heads, qk_dim)
        o = mla_vup(o_lat.transpose(1, 0, 2), l, wv2)
        merged = merge(y_ssm, o, z, l, bd, 512, ga_col0, gb_col0, w_sso_b, w_ao_b)
        x1, h2 = oproj(st_s, merged, xs, l, w_o_b, g_ffn3)
        buf0, buf1 = state_conv[l, :, 0, :], state_conv[l, :, 1, :]
        act, ug, uv = ffn_up_sample(h2, l, w_up_b, conv_w, conv_b3, buf0, buf1)
        xs = ffn_down(st_s, act, x1, l, w_down_b)
        outs["kv_s"].append(rows_s.reshape(bd, 1, kv_lora + rope))
        outs["re_s"].append(hre.reshape(bd, n_groups, state_p))
        outs["im_s"].append(him.reshape(bd, n_groups, state_p))
        outs["cv_s"].append(jnp.stack([buf1, jnp.concatenate([ug, uv], axis=-1)], axis=1))

    st = lambda key: jnp.stack(outs[key])
    return (xp.reshape(batch, seq, d), xs.reshape(bd, 1, d), st("kv_p"), st("kv_s"),
            st("re_p"), st("im_p"), st("re_s"), st("im_s"), st("cv_p"), st("cv_s"))
```
